```python
import math
import jax
import jax.numpy as jnp
from jax import lax
import numpy as np

D_MODEL = 1024
BATCH = 8
SEQ = 8192
DEPTH = 1
DEC_BATCH = 128
DEC_SEQ = 8
PAST_LEN = 8192
PAGE_SIZE = 128

H_A = 4
D_QK = 64
D_V_A = 2 * D_QK
W_A = H_A * D_V_A
H_B = 4
D_K_B = 128
D_V_B = 128
W_B = H_B * D_V_B
D_FF = 2816
IN_WIDTHS = (H_A * 2 * D_QK, H_A * 2 * D_QK, W_A, H_B * D_K_B, H_B * D_K_B, W_B, W_B, D_MODEL, D_MODEL)
D_IN = 2 * H_A * 2 * D_QK + W_A + 2 * H_B * D_K_B + 2 * W_B + 2 * D_MODEL
Q_BLOCK = 128
CHUNK = 64
EPS = 1e-6
NEG_INF = -1e30

kernel_name = 'hybrid_diffattn_hgrn2_step'


def rms_norm(x, g):
    xf = x.astype(jnp.float32)
    y = xf * lax.rsqrt(jnp.mean(xf * xf, axis=-1, keepdims=True) + EPS)
    return (y * g.astype(jnp.float32)).astype(x.dtype)


def swiglu_ffn(x, w_gu, w_down):
    g, u = jnp.split(x @ w_gu, 2, axis=-1)
    return (jax.nn.silu(g) * u) @ w_down


def lambda_init(layer):
    return 0.8 - 0.6 * math.exp(-0.3 * layer)


def alibi_slopes(n):
    return jnp.asarray([2.0 ** (-8.0 * (h + 1) / n) for h in range(n)], jnp.float32)


def diff_attn_core(q, qpos, segments, lam):
    slopes = alibi_slopes(H_A)
    qf = q.astype(jnp.float32) * (D_QK ** -0.5)
    scores = []
    for k, _, kpos in segments:
        s = jnp.einsum('bqhmd,bkhmd->bhmqk', qf, k.astype(jnp.float32))
        dist = (qpos[:, None] - kpos[None, :]).astype(jnp.float32)
        bias = -slopes[:, None, None] * dist
        causal = kpos[None, :] <= qpos[:, None]
        scores.append(jnp.where(causal, s + bias[None, :, None], NEG_INF))
    p = jax.nn.softmax(jnp.concatenate(scores, axis=-1), axis=-1)
    w = p[:, :, 0] - lam * p[:, :, 1]
    outs = []
    off = 0
    for _, v, kpos in segments:
        n = kpos.shape[0]
        outs.append(jnp.einsum('bhqk,bkhd->bqhd', w[..., off:off + n], v.astype(jnp.float32)))
        off += n
    out = outs[0]
    for o in outs[1:]:
        out = out + o
    return out


def diff_attn_blocked(q, k, v, lam):
    B, L = q.shape[0], q.shape[1]
    qb = math.gcd(L, Q_BLOCK)
    nb = L // qb
    q_blocks = jnp.moveaxis(q.reshape(B, nb, qb, H_A, 2, D_QK), 1, 0)
    kpos = jnp.arange(L)

    def one_block(args):
        q_blk, i = args
        qpos = i * qb + jnp.arange(qb)
        return diff_attn_core(q_blk, qpos, [(k, v, kpos)], lam)

    out = lax.map(one_block, (q_blocks, jnp.arange(nb)))
    return jnp.moveaxis(out, 0, 1).reshape(B, L, H_A, D_V_A)


def hgrn2_chunked(q, k, v, logf, s0):
    B, L = q.shape[0], q.shape[1]
    c = math.gcd(L, CHUNK)
    n = L // c

    def to_chunks(a):
        return jnp.moveaxis(a.astype(jnp.float32).reshape(B, n, c, *a.shape[2:]), 1, 0)

    causal = jnp.tril(jnp.ones((c, c), bool))[None, :, :, None, None]

    def step(S, inp):
        qc, kc, vc, gc = inp
        b = jnp.cumsum(gc, axis=1)
        o_inter = jnp.einsum('bthk,bhkv->bthv', qc * jnp.exp(b), S)
        decay = jnp.exp(jnp.where(causal, b[:, :, None] - b[:, None, :], -jnp.inf))
        a = jnp.sum(qc[:, :, None] * decay * kc[:, None], axis=-1)
        o_intra = jnp.einsum('btsh,bshv->bthv', a, vc)
        b_last = b[:, -1]
        S_new = jnp.exp(b_last)[..., None] * S + jnp.einsum(
            'bshk,bshv->bhkv', kc * jnp.exp(b_last[:, None] - b), vc)
        return S_new, o_inter + o_intra

    S, o = lax.scan(step, s0.astype(jnp.float32), (to_chunks(q), to_chunks(k), to_chunks(v), to_chunks(logf)))
    return jnp.moveaxis(o, 0, 1).reshape(B, L, H_B, D_V_B), S


def layer_forward(x, p, lb, lam, lam_init, s0, past):
    B, L, _ = x.shape
    h = x + 0.5 * rms_norm(swiglu_ffn(rms_norm(x, p['ffn1_pre_g']), p['ffn1_w_gu'], p['ffn1_w_down']), p['ffn1_post_g'])
    u = rms_norm(h, p['mix_pre_g'])
    z = u @ p['w_in']
    splits = [int(s) for s in np.cumsum(IN_WIDTHS)[:-1]]
    q_a, k_a, v_a, q_b, f_b, i_b, g_b, gate_a, gate_b = jnp.split(z, splits, axis=-1)
    qa = q_a.reshape(B, L, H_A, 2, D_QK)
    ka = k_a.reshape(B, L, H_A, 2, D_QK)
    va = v_a.reshape(B, L, H_A, D_V_A)
    if past is None:
        o_a = diff_attn_blocked(qa, ka, va, lam)
    else:
        k_past, v_past = past
        P = k_past.shape[1]
        qpos = P + jnp.arange(L)
        o_a = diff_attn_core(qa, qpos, [(k_past, v_past, jnp.arange(P)), (ka, va, qpos)], lam)
    o_a = (rms_norm(o_a, p['attn_subln_g']) * (1.0 - lam_init)).reshape(B, L, W_A)
    qb_ = jax.nn.silu(q_b.astype(jnp.float32)).reshape(B, L, H_B, D_K_B)
    fr = f_b.astype(jnp.float32).reshape(B, L, H_B, D_K_B)
    f = lb + (1.0 - lb) * jax.nn.sigmoid(fr)
    kb = (1.0 - lb) * jax.nn.sigmoid(-fr)
    vb = i_b.reshape(B, L, H_B, D_V_B)
    o_b, s_new = hgrn2_chunked(qb_, kb, vb, jnp.log(f), s0)
    o_b = (rms_norm(o_b, p['hgrn_norm_g']) * jax.nn.silu(g_b.astype(jnp.float32).reshape(B, L, H_B, D_V_B))).reshape(B, L, W_B)
    m = (jax.nn.sigmoid(gate_a) * (o_a.astype(x.dtype) @ p['w_proj_a'])
         + jax.nn.sigmoid(gate_b) * (o_b.astype(x.dtype) @ p['w_proj_b']))
    h = h + rms_norm(m @ p['w_out'], p['mix_post_g'])
    y = h + 0.5 * rms_norm(swiglu_ffn(rms_norm(h, p['ffn2_pre_g']), p['ffn2_w_gu'], p['ffn2_w_down']), p['ffn2_post_g'])
    return y, ka.reshape(B, L, H_A, 2 * D_QK), va, s_new.astype(x.dtype)


def setup_inputs(seed: int = 0) -> dict:
    key = jax.random.key(seed)
    keys = iter(jax.random.split(key, 40))

    def nrm(shape, scale):
        return scale * jax.random.normal(next(keys), shape, jnp.float32)

    def gain(shape):
        return 1.0 + nrm(shape, 0.02)

    n_pages = PAST_LEN // PAGE_SIZE
    n_used = DEC_BATCH * n_pages
    n_pool = n_used + max(1, n_used // 4)
    x_prompt = nrm((BATCH, SEQ, D_MODEL), 1.0)
    x_sample = nrm((DEC_BATCH, DEC_SEQ, D_MODEL), 1.0)
    cache_k = nrm((DEPTH, n_pool, PAGE_SIZE, H_A, 2 * D_QK), 1.0)
    cache_v = nrm((DEPTH, n_pool, PAGE_SIZE, H_A, D_V_A), 1.0)
    state_s = nrm((DEPTH, DEC_BATCH, H_B, D_K_B, D_V_B), 0.5)
    perm = jax.random.permutation(next(keys), n_pool)
    page_table = perm[:n_used].reshape(DEC_BATCH, n_pages).astype(jnp.int32)
    return {
        'x_prompt': x_prompt,
        'x_sample': x_sample,
        'cache_k': cache_k,
        'cache_v': cache_v,
        'state_s': state_s,
        'page_table': page_table,
        'ffn1_pre_g': gain((DEPTH, D_MODEL)),
        'ffn1_post_g': gain((DEPTH, D_MODEL)),
        'ffn1_w_gu': nrm((DEPTH, D_MODEL, 2 * D_FF), D_MODEL ** -0.5),
        'ffn1_w_down': nrm((DEPTH, D_FF, D_MODEL), D_FF ** -0.5),
        'mix_pre_g': gain((DEPTH, D_MODEL)),
        'mix_post_g': gain((DEPTH, D_MODEL)),
        'w_in': nrm((DEPTH, D_MODEL, D_IN), D_MODEL ** -0.5),
        'lambda_q1': nrm((DEPTH, D_QK), 0.1),
        'lambda_k1': nrm((DEPTH, D_QK), 0.1),
        'lambda_q2': nrm((DEPTH, D_QK), 0.1),
        'lambda_k2': nrm((DEPTH, D_QK), 0.1),
        'attn_subln_g': gain((DEPTH, D_V_A)),
        'hgrn_lb_logits': nrm((DEPTH + 1, H_B * D_K_B), 0.1),
        'hgrn_norm_g': gain((DEPTH, D_V_B)),
        'w_proj_a': nrm((DEPTH, W_A, D_MODEL), W_A ** -0.5),
        'w_proj_b': nrm((DEPTH, W_B, D_MODEL), W_B ** -0.5),
        'w_out': nrm((DEPTH, D_MODEL, D_MODEL), D_MODEL ** -0.5),
        'ffn2_pre_g': gain((DEPTH, D_MODEL)),
        'ffn2_post_g': gain((DEPTH, D_MODEL)),
        'ffn2_w_gu': nrm((DEPTH, D_MODEL, 2 * D_FF), D_MODEL ** -0.5),
        'ffn2_w_down': nrm((DEPTH, D_FF, D_MODEL), D_FF ** -0.5),
    }


def reference(x_prompt, x_sample, cache_k, cache_v, state_s, page_table,
              ffn1_pre_g, ffn1_post_g, ffn1_w_gu, ffn1_w_down,
              mix_pre_g, mix_post_g, w_in,
              lambda_q1, lambda_k1, lambda_q2, lambda_k2, attn_subln_g,
              hgrn_lb_logits, hgrn_norm_g, w_proj_a, w_proj_b, w_out,
              ffn2_pre_g, ffn2_post_g, ffn2_w_gu, ffn2_w_down):
    lb_all = jnp.cumsum(jax.nn.softmax(hgrn_lb_logits.astype(jnp.float32), axis=0), axis=0)
    dec_b = x_sample.shape[0]
    n_pages = page_table.shape[1]
    past_len = n_pages * PAGE_SIZE
    yp, ys = x_prompt, x_sample
    kp_l, vp_l, sp_l, ks_l, vs_l, ss_l = [], [], [], [], [], []
    for l in range(DEPTH):
        p = {
            'ffn1_pre_g': ffn1_pre_g[l], 'ffn1_post_g': ffn1_post_g[l],
            'ffn1_w_gu': ffn1_w_gu[l], 'ffn1_w_down': ffn1_w_down[l],
            'mix_pre_g': mix_pre_g[l], 'mix_post_g': mix_post_g[l], 'w_in': w_in[l],
            'attn_subln_g': attn_subln_g[l], 'hgrn_norm_g': hgrn_norm_g[l],
            'w_proj_a': w_proj_a[l], 'w_proj_b': w_proj_b[l], 'w_out': w_out[l],
            'ffn2_pre_g': ffn2_pre_g[l], 'ffn2_post_g': ffn2_post_g[l],
            'ffn2_w_gu': ffn2_w_gu[l], 'ffn2_w_down': ffn2_w_down[l],
        }
        lam_i = lambda_init(l)
        lam = (jnp.exp(jnp.sum(lambda_q1[l].astype(jnp.float32) * lambda_k1[l].astype(jnp.float32)))
               - jnp.exp(jnp.sum(lambda_q2[l].astype(jnp.float32) * lambda_k2[l].astype(jnp.float32)))
               + lam_i)
        lb = lb_all[l].reshape(H_B, D_K_B)
        s0_p = jnp.zeros((yp.shape[0], H_B, D_K_B, D_V_B), jnp.float32)
        yp, kp, vp, sp = layer_forward(yp, p, lb, lam, lam_i, s0_p, None)
        k_past = cache_k[l, page_table].reshape(dec_b, past_len, H_A, 2, D_QK)
        v_past = cache_v[l, page_table].reshape(dec_b, past_len, H_A, D_V_A)
        ys, ks, vs, ss = layer_forward(ys, p, lb, lam, lam_i, state_s[l], (k_past, v_past))
        kp_l.append(kp); vp_l.append(vp); sp_l.append(sp)
        ks_l.append(ks); vs_l.append(vs); ss_l.append(ss)
    return (yp, ys, jnp.stack(kp_l), jnp.stack(vp_l), jnp.stack(sp_l),
            jnp.stack(ks_l), jnp.stack(vs_l), jnp.stack(ss_l))
```

```python
import functools
import math

import numpy as np
import jax
import jax.numpy as jnp
from jax import lax
from jax.experimental import pallas as pl
from jax.experimental.pallas import tpu as pltpu

F32 = jnp.float32
BF16 = jnp.bfloat16
EPS = 1e-6
NEG_INF = -1e30
PAGE_SIZE = 128
LANES = 128
SUB_CHUNK = 16
EXP_CLAMP = 80.0
VMEM_LIMIT = 56 * 1024 * 1024
FFN_ROWS = 1024
PROJ_ROWS = 512
MERGE_ROWS = 1024
ATTN_Q_ROWS = 512
ATTN_KV_ROWS = 512
HGRN_CHUNK = 128
DECODE_PAGES = 16


def _cparams(sem):
    return pltpu.CompilerParams(dimension_semantics=sem, vmem_limit_bytes=VMEM_LIMIT)


def _resident(shape):
    nd = len(shape)
    return pl.BlockSpec(shape, lambda *_: (0,) * nd, pipeline_mode=pl.Buffered(1))


def _rms(x, g):
    return x * lax.rsqrt(jnp.mean(x * x, axis=-1, keepdims=True) + EPS) * g


def _dot(a, b):
    return jnp.dot(a, b, preferred_element_type=F32)


def _dot_nt(a, b):
    return lax.dot_general(a, b, (((1,), (1,)), ((), ())), preferred_element_type=F32)


def _dot_tn(a, b):
    return lax.dot_general(a, b, (((0,), (0,)), ((), ())), preferred_element_type=F32)


def _pick_tile(n, target):
    t = math.gcd(n, target)
    assert t % 8 == 0 or t == n, (n, target)
    return t


def _ffn_kernel(x_ref, pre_ref, post_ref, wgu_ref, wdn_ref, o_ref, *, d_ff, ck):
    x = x_ref[...]
    xn = _rms(x, pre_ref[...]).astype(BF16)
    acc = None
    for c in range(d_ff // ck):
        g = _dot(xn, wgu_ref[:, c * ck:(c + 1) * ck])
        u = _dot(xn, wgu_ref[:, d_ff + c * ck:d_ff + (c + 1) * ck])
        a = (g * jax.nn.sigmoid(g) * u).astype(BF16)
        part = _dot(a, wdn_ref[c * ck:(c + 1) * ck, :])
        acc = part if acc is None else acc + part
    o_ref[...] = x + 0.5 * _rms(acc, post_ref[...])


def _ffn(x, pre_g, post_g, w_gu, w_down):
    n, d = x.shape
    d_ff = w_down.shape[0]
    tm = _pick_tile(n, FFN_ROWS)
    ck = 2 * LANES if d_ff % (2 * LANES) == 0 else d_ff
    row = pl.BlockSpec((tm, d), lambda i: (i, 0))
    return pl.pallas_call(
        functools.partial(_ffn_kernel, d_ff=d_ff, ck=ck),
        grid=(n // tm,),
        in_specs=[row, _resident((1, d)), _resident((1, d)), _resident(w_gu.shape), _resident(w_down.shape)],
        out_specs=row,
        out_shape=jax.ShapeDtypeStruct((n, d), F32),
        compiler_params=_cparams(("parallel",)),
        name="ffn",
    )(x, pre_g, post_g, w_gu, w_down)


def _proj_kernel(h_ref, g_ref, win_ref, lbl_ref,
                 qa_ref, ka_ref, ka16_ref, va_ref, va16_ref,
                 qb_ref, lf_ref, kb_ref, vb_ref, sg_ref, ga_ref, gb_ref, *, offs, layer, q_scale):
    u = _rms(h_ref[...], g_ref[...]).astype(BF16)

    def z(i):
        return _dot(u, win_ref[:, offs[i]:offs[i + 1]])

    qa_ref[...] = (z(0) * q_scale).astype(BF16)
    k = z(1)
    ka_ref[...] = k
    ka16_ref[...] = k.astype(BF16)
    v = z(2)
    va_ref[...] = v
    va16_ref[...] = v.astype(BF16)
    qb = z(3)
    qb_ref[...] = (qb * jax.nn.sigmoid(qb)).astype(BF16)
    lbl = lbl_ref[...]
    e = jnp.exp(lbl - jnp.max(lbl, axis=0, keepdims=True))
    lb = jnp.sum(e[:layer + 1], axis=0, keepdims=True) / jnp.sum(e, axis=0, keepdims=True)
    fr = z(4)
    lf_ref[...] = jnp.log(lb + (1.0 - lb) * jax.nn.sigmoid(fr))
    kb_ref[...] = ((1.0 - lb) * jax.nn.sigmoid(-fr)).astype(BF16)
    vb_ref[...] = z(5).astype(BF16)
    gbr = z(6)
    sg_ref[...] = (gbr * jax.nn.sigmoid(gbr)).astype(BF16)
    ga_ref[...] = jax.nn.sigmoid(z(7)).astype(BF16)
    gb_ref[...] = jax.nn.sigmoid(z(8)).astype(BF16)


def _proj(h, mix_pre_g, w_in, lb_logits, widths, layer, q_scale):
    n, d = h.shape
    tm = _pick_tile(n, PROJ_ROWS)
    offs = tuple(int(o) for o in np.concatenate([[0], np.cumsum(widths)]))
    w = widths

    def out(width, dtype):
        return pl.BlockSpec((tm, width), lambda i: (i, 0)), jax.ShapeDtypeStruct((n, width), dtype)

    outs = [out(w[0], BF16), out(w[1], F32), out(w[1], BF16), out(w[2], F32), out(w[2], BF16),
            out(w[3], BF16), out(w[4], F32), out(w[4], BF16), out(w[5], BF16), out(w[6], BF16),
            out(w[7], BF16), out(w[8], BF16)]
    return pl.pallas_call(
        functools.partial(_proj_kernel, offs=offs, layer=layer, q_scale=q_scale),
        grid=(n // tm,),
        in_specs=[pl.BlockSpec((tm, d), lambda i: (i, 0)), _resident((1, d)), _resident(w_in.shape),
                  _resident(lb_logits.shape)],
        out_specs=[o[0] for o in outs],
        out_shape=[o[1] for o in outs],
        compiler_params=_cparams(("parallel",)),
        name="proj",
    )(h, mix_pre_g, w_in, lb_logits)


def _alibi_slope(h, n_heads):
    slope = jnp.float32(0.0)
    for hh in range(n_heads):
        slope = jnp.where(h == hh, jnp.float32(2.0 ** (-8.0 * (hh + 1) / n_heads)), slope)
    return slope


def _lambda(lam_ref, lam_init):
    lp = lam_ref[...]
    l1 = jnp.exp(jnp.sum(lp[0:1] * lp[1:2], axis=-1, keepdims=True))
    l2 = jnp.exp(jnp.sum(lp[2:3] * lp[3:4], axis=-1, keepdims=True))
    return l1 - l2 + lam_init


def _split_maps(x, d_qk):
    lane = lax.broadcasted_iota(jnp.int32, x.shape, 1)
    zero = jnp.zeros_like(x)
    return jnp.where(lane < d_qk, x, zero), jnp.where(lane >= d_qk, x, zero)


def _softmax_update(s, v, m_prev, l_prev, acc_prev):
    m_new = jnp.maximum(m_prev, jnp.max(s, axis=-1, keepdims=True))
    alpha = jnp.exp(m_prev - m_new)
    p = jnp.exp(s - m_new)
    l_new = alpha * l_prev + jnp.sum(p, axis=-1, keepdims=True)
    acc_new = alpha * acc_prev + _dot(p.astype(BF16), v)
    return m_new, l_new, acc_new


def _diff_finish(acc, l, lam, g, out_scale, rows):
    o = acc[:rows] / l[:rows] - lam * (acc[rows:] / l[rows:])
    return _rms(o, g) * out_scale


def _attn_kernel(qi_ref, kj_ref, q_ref, k_ref, v_ref, lam_ref, g_ref, o_ref, m_sc, l_sc, acc_sc,
                 *, tq, tk, d_qk, n_heads, lam_init):
    h = pl.program_id(1)
    t = pl.program_id(2)
    i = qi_ref[t]
    j = kj_ref[t]
    last_j = ((i + 1) * tq - 1) // tk

    @pl.when(j == 0)
    def _():
        m_sc[...] = jnp.full(m_sc.shape, NEG_INF, F32)
        l_sc[...] = jnp.zeros(l_sc.shape, F32)
        acc_sc[...] = jnp.zeros(acc_sc.shape, F32)

    q = q_ref[0]
    k1, k2 = _split_maps(k_ref[0], d_qk)
    s = jnp.concatenate([_dot_nt(q, k1), _dot_nt(q, k2)], axis=0)
    kpos = j * tk + lax.broadcasted_iota(jnp.int32, (1, tk), 1)
    bias = _alibi_slope(h, n_heads) * (kpos - i * tq).astype(F32)
    s = s + bias

    def update(s):
        m_new, l_new, acc_new = _softmax_update(s, v_ref[0], m_sc[...], l_sc[...], acc_sc[...])
        m_sc[...] = m_new
        l_sc[...] = l_new
        acc_sc[...] = acc_new

    needs_mask = (j + 1) * tk - 1 > i * tq

    @pl.when(needs_mask)
    def _():
        r = lax.broadcasted_iota(jnp.int32, (2 * tq, 1), 0)
        qpos = i * tq + jnp.where(r >= tq, r - tq, r)
        update(jnp.where(kpos <= qpos, s, NEG_INF))

    @pl.when(jnp.logical_not(needs_mask))
    def _():
        update(s)

    @pl.when(j == last_j)
    def _():
        o = _diff_finish(acc_sc[...], l_sc[...], _lambda(lam_ref, lam_init), g_ref[...], 1.0 - lam_init, tq)
        o_ref[0] = o.astype(o_ref.dtype)


def _attn_prompt(qa, ka, va, lam_params, subln_g, *, n_heads, d_qk, lam_init):
    b, l, w = qa.shape
    dh = w // n_heads
    tq = _pick_tile(l, ATTN_Q_ROWS)
    tk = _pick_tile(l, ATTN_KV_ROWS)
    pairs = [(i, j) for i in range(l // tq) for j in range(((i + 1) * tq - 1) // tk + 1)]
    qi = jnp.asarray([p[0] for p in pairs], jnp.int32)
    kj = jnp.asarray([p[1] for p in pairs], jnp.int32)
    grid_spec = pltpu.PrefetchScalarGridSpec(
        num_scalar_prefetch=2,
        grid=(b, n_heads, len(pairs)),
        in_specs=[
            pl.BlockSpec((1, tq, dh), lambda bb, hh, t, qi, kj: (bb, qi[t], hh)),
            pl.BlockSpec((1, tk, dh), lambda bb, hh, t, qi, kj: (bb, kj[t], hh)),
            pl.BlockSpec((1, tk, dh), lambda bb, hh, t, qi, kj: (bb, kj[t], hh)),
            pl.BlockSpec(lam_params.shape, lambda bb, hh, t, qi, kj: (0, 0)),
            pl.BlockSpec(subln_g.shape, lambda bb, hh, t, qi, kj: (0, 0)),
        ],
        out_specs=pl.BlockSpec((1, tq, dh), lambda bb, hh, t, qi, kj: (bb, qi[t], hh)),
        scratch_shapes=[pltpu.VMEM((2 * tq, 1), F32), pltpu.VMEM((2 * tq, 1), F32),
                        pltpu.VMEM((2 * tq, dh), F32)],
    )
    return pl.pallas_call(
        functools.partial(_attn_kernel, tq=tq, tk=tk, d_qk=d_qk, n_heads=n_heads, lam_init=lam_init),
        grid_spec=grid_spec,
        out_shape=jax.ShapeDtypeStruct((b, l, w), BF16),
        compiler_params=_cparams(("parallel", "parallel", "arbitrary")),
        name="attn_prompt",
    )(qi, kj, qa, ka, va, lam_params, subln_g)


def _attn_decode_kernel(pt_ref, q_ref, kn_ref, vn_ref, lam_ref, g_ref, *rest,
                        pp, n_heads, d_qk, lam_init, past_len):
    k_refs = rest[:pp]
    v_refs = rest[pp:2 * pp]
    o_ref, m_sc, l_sc, acc_sc = rest[2 * pp:]
    c = pl.program_id(1)
    nq = q_ref.shape[1]
    dh = q_ref.shape[2] // n_heads
    page = k_refs[0].shape[1]

    @pl.when(c == 0)
    def _():
        m_sc[...] = jnp.full(m_sc.shape, NEG_INF, F32)
        l_sc[...] = jnp.zeros(l_sc.shape, F32)
        acc_sc[...] = jnp.zeros(acc_sc.shape, F32)

    def q_rows(h):
        qh = q_ref[0][:, h * dh:(h + 1) * dh].astype(F32)
        q1, q2 = _split_maps(qh, d_qk)
        return jnp.concatenate([q1, q2], axis=0).astype(BF16)

    for h in range(n_heads):
        hs = slice(h * dh, (h + 1) * dh)
        slope = jnp.float32(2.0 ** (-8.0 * (h + 1) / n_heads))
        qh = q_rows(h)
        s = jnp.concatenate([_dot_nt(qh, k_refs[p][0][:, hs].astype(BF16)) for p in range(pp)], axis=1)
        kpos = c * (pp * page) + lax.broadcasted_iota(jnp.int32, (1, pp * page), 1)
        s = s + slope * (kpos - past_len).astype(F32)
        m_prev, l_prev, acc_prev = m_sc[h], l_sc[h], acc_sc[h]
        m_new = jnp.maximum(m_prev, jnp.max(s, axis=-1, keepdims=True))
        alpha = jnp.exp(m_prev - m_new)
        p_all = jnp.exp(s - m_new)
        l_sc[h] = alpha * l_prev + jnp.sum(p_all, axis=-1, keepdims=True)
        p16 = p_all.astype(BF16)
        pv = None
        for p in range(pp):
            part = _dot(p16[:, p * page:(p + 1) * page], v_refs[p][0][:, hs].astype(BF16))
            pv = part if pv is None else pv + part
        acc_sc[h] = alpha * acc_prev + pv
        m_sc[h] = m_new

    @pl.when(c == pl.num_programs(1) - 1)
    def _():
        lam = _lambda(lam_ref, lam_init)
        outs = []
        for h in range(n_heads):
            hs = slice(h * dh, (h + 1) * dh)
            slope = jnp.float32(2.0 ** (-8.0 * (h + 1) / n_heads))
            pad = jnp.zeros((page - nq, dh), F32)
            kn = jnp.concatenate([kn_ref[0][:, hs], pad], axis=0).astype(BF16)
            vn = jnp.concatenate([vn_ref[0][:, hs], pad], axis=0).astype(BF16)
            s = _dot_nt(q_rows(h), kn)
            jj = lax.broadcasted_iota(jnp.int32, (1, page), 1)
            r = lax.broadcasted_iota(jnp.int32, (2 * nq, 1), 0)
            ii = jnp.where(r >= nq, r - nq, r)
            s = jnp.where(jj <= ii, s + slope * jj.astype(F32), NEG_INF)
            _, l_new, acc_new = _softmax_update(s, vn, m_sc[h], l_sc[h], acc_sc[h])
            outs.append(_diff_finish(acc_new, l_new, lam, g_ref[...], 1.0 - lam_init, nq))
        o_ref[0] = jnp.concatenate(outs, axis=1).astype(o_ref.dtype)


def _attn_decode(qa, k_new, v_new, cache_k, cache_v, page_table, lam_params, subln_g,
                 *, n_heads, d_qk, lam_init):
    b, nq, w = qa.shape
    n_pages = page_table.shape[1]
    pp = math.gcd(n_pages, DECODE_PAGES)
    page = cache_k.shape[1]

    def page_spec(p):
        return pl.BlockSpec((1, page, w), lambda bb, c, pt: (pt[bb * n_pages + c * pp + p], 0, 0))

    def tok_spec():
        return pl.BlockSpec((1, nq, w), lambda bb, c, pt: (bb, 0, 0))

    grid_spec = pltpu.PrefetchScalarGridSpec(
        num_scalar_prefetch=1,
        grid=(b, n_pages // pp),
        in_specs=[tok_spec(), tok_spec(), tok_spec(),
                  pl.BlockSpec(lam_params.shape, lambda bb, c, pt: (0, 0)),
                  pl.BlockSpec(subln_g.shape, lambda bb, c, pt: (0, 0))]
                 + [page_spec(p) for p in range(pp)] + [page_spec(p) for p in range(pp)],
        out_specs=tok_spec(),
        scratch_shapes=[pltpu.VMEM((n_heads, 2 * nq, 1), F32), pltpu.VMEM((n_heads, 2 * nq, 1), F32),
                        pltpu.VMEM((n_heads, 2 * nq, w // n_heads), F32)],
    )
    return pl.pallas_call(
        functools.partial(_attn_decode_kernel, pp=pp, n_heads=n_heads, d_qk=d_qk, lam_init=lam_init,
                          past_len=n_pages * page),
        grid_spec=grid_spec,
        out_shape=jax.ShapeDtypeStruct((b, nq, w), BF16),
        compiler_params=_cparams(("parallel", "arbitrary")),
        name="attn_decode",
    )(page_table.reshape(-1), qa, k_new, v_new, lam_params, subln_g,
      *([cache_k] * pp), *([cache_v] * pp))


def _cumsum_rows(x):
    c = x.shape[0]
    tri = (lax.broadcasted_iota(jnp.int32, (c, c), 0) >= lax.broadcasted_iota(jnp.int32, (c, c), 1))
    tri = jnp.where(tri, 1.0, 0.0).astype(BF16)
    hi = x.astype(BF16)
    r1 = x - hi.astype(F32)
    mid = r1.astype(BF16)
    lo = (r1 - mid.astype(F32)).astype(BF16)
    return _dot(tri, hi) + _dot(tri, mid) + _dot(tri, lo)


def _hgrn_chunk(q, k, v16, lf, st):
    c = q.shape[0]
    b = _cumsum_rows(lf)
    o = _dot_nt((q * jnp.exp(b)).astype(BF16), st.astype(BF16))
    t_sub = min(SUB_CHUNK, c)
    parts = []
    for i in range(c // t_sub):
        lo, hi = i * t_sub, (i + 1) * t_sub
        anchor = b[lo + t_sub // 2 - 1:lo + t_sub // 2, :]
        qi = (q[lo:hi] * jnp.exp(jnp.minimum(b[lo:hi] - anchor, EXP_CLAMP))).astype(BF16)
        ki = (k * jnp.exp(jnp.minimum(anchor - b, EXP_CLAMP))).astype(BF16)
        a = _dot_nt(qi, ki)
        tt = lo + lax.broadcasted_iota(jnp.int32, a.shape, 0)
        ss = lax.broadcasted_iota(jnp.int32, a.shape, 1)
        a = jnp.where(ss <= tt, a, 0.0)
        parts.append(_dot(a.astype(BF16), v16))
    o = o + jnp.concatenate(parts, axis=0)
    b_last = b[c - 1:c, :]
    kd = (k * jnp.exp(b_last - b)).astype(BF16)
    st_new = jnp.exp(b_last) * st + _dot_tn(v16, kd)
    return o, st_new


def _hgrn_kernel(q_ref, k_ref, v_ref, lf_ref, sg_ref, g_ref, o_ref, s_ref, st_sc, *, n_heads):
    c_idx = pl.program_id(1)
    dk = q_ref.shape[2] // n_heads
    dv = v_ref.shape[2] // n_heads

    @pl.when(c_idx == 0)
    def _():
        st_sc[...] = jnp.zeros(st_sc.shape, F32)

    outs = []
    for h in range(n_heads):
        ks = slice(h * dk, (h + 1) * dk)
        vs = slice(h * dv, (h + 1) * dv)
        o, st_new = _hgrn_chunk(q_ref[0][:, ks].astype(F32), k_ref[0][:, ks].astype(F32),
                                v_ref[0][:, vs], lf_ref[0][:, ks], st_sc[h])
        st_sc[h] = st_new
        outs.append(_rms(o, g_ref[...]) * sg_ref[0][:, vs].astype(F32))
    o_ref[0] = jnp.concatenate(outs, axis=1).astype(o_ref.dtype)

    @pl.when(c_idx == pl.num_programs(1) - 1)
    def _():
        for h in range(n_heads):
            s_ref[0, h] = st_sc[h].T


def _hgrn_prompt(qb, kb, vb, lf, sg, norm_g, *, n_heads):
    b, l, wk = qb.shape
    wv = vb.shape[2]
    c = _pick_tile(l, HGRN_CHUNK)
    dk, dv = wk // n_heads, wv // n_heads

    def tok(width):
        return pl.BlockSpec((1, c, width), lambda bb, cc: (bb, cc, 0))

    return pl.pallas_call(
        functools.partial(_hgrn_kernel, n_heads=n_heads),
        grid=(b, l // c),
        in_specs=[tok(wk), tok(wk), tok(wv), tok(wk), tok(wv), pl.BlockSpec(norm_g.shape, lambda bb, cc: (0, 0))],
        out_specs=[tok(wv), pl.BlockSpec((1, n_heads, dk, dv), lambda bb, cc: (bb, 0, 0, 0))],
        out_shape=[jax.ShapeDtypeStruct((b, l, wv), BF16), jax.ShapeDtypeStruct((b, n_heads, dk, dv), F32)],
        scratch_shapes=[pltpu.VMEM((n_heads, dv, dk), F32)],
        compiler_params=_cparams(("parallel", "arbitrary")),
        name="hgrn_prompt",
    )(qb, kb, vb, lf, sg, norm_g)


def _hgrn_decode_kernel(q_ref, k_ref, v_ref, lf_ref, sg_ref, g_ref, s0_ref, o_ref, s_ref, *, n_heads):
    n = q_ref.shape[1]
    dk = q_ref.shape[2] // n_heads
    dv = v_ref.shape[2] // n_heads
    row = lax.broadcasted_iota(jnp.int32, (n, 1), 0)
    pad = 2 * 8 - n if n < 2 * 8 else 0
    outs = []
    for h in range(n_heads):
        ks = slice(h * dk, (h + 1) * dk)
        vs = slice(h * dv, (h + 1) * dv)
        q = q_ref[0][:, ks].astype(F32)
        k = k_ref[0][:, ks].astype(F32)
        v = v_ref[0][:, vs].astype(F32)
        lf = lf_ref[0][:, ks]
        st = s0_ref[0, h].T
        b = lf
        for sh in [1 << e for e in range((n - 1).bit_length())]:
            shifted = jnp.concatenate([jnp.zeros((sh, dk), F32), b[:n - sh]], axis=0)
            b = b + shifted

        def padded(x):
            return jnp.concatenate([x, jnp.zeros((pad, x.shape[1]), F32)], axis=0) if pad else x

        o = _dot_nt(padded(q * jnp.exp(b)).astype(BF16), st.astype(BF16))[:n]
        for s in range(n):
            w = jnp.sum(q * k[s:s + 1] * jnp.exp(jnp.minimum(b - b[s:s + 1], 0.0)), axis=-1, keepdims=True)
            o = o + jnp.where(row >= s, w, 0.0) * v[s:s + 1]
        b_last = b[n - 1:n]
        kd = padded(k * jnp.exp(b_last - b)).astype(BF16)
        st_new = jnp.exp(b_last) * st + _dot_tn(padded(v).astype(BF16), kd)
        s_ref[0, h] = st_new.T
        outs.append(_rms(o, g_ref[...]) * sg_ref[0][:, vs].astype(F32))
    o_ref[0] = jnp.concatenate(outs, axis=1).astype(o_ref.dtype)


def _hgrn_decode(qb, kb, vb, lf, sg, norm_g, s0, *, n_heads):
    b, n, wk = qb.shape
    wv = vb.shape[2]
    dk, dv = wk // n_heads, wv // n_heads

    def tok(width):
        return pl.BlockSpec((1, n, width), lambda bb: (bb, 0, 0))

    st = pl.BlockSpec((1, n_heads, dk, dv), lambda bb: (bb, 0, 0, 0))
    return pl.pallas_call(
        functools.partial(_hgrn_decode_kernel, n_heads=n_heads),
        grid=(b,),
        in_specs=[tok(wk), tok(wk), tok(wv), tok(wk), tok(wv), pl.BlockSpec(norm_g.shape, lambda bb: (0, 0)), st],
        out_specs=[tok(wv), st],
        out_shape=[jax.ShapeDtypeStruct((b, n, wv), BF16), jax.ShapeDtypeStruct((b, n_heads, dk, dv), F32)],
        compiler_params=_cparams(("parallel",)),
        name="hgrn_decode",
    )(qb, kb, vb, lf, sg, norm_g, s0)


def _merge_kernel(h_ref, oa_ref, ob_ref, ga_ref, gb_ref, wa_ref, wb_ref, wo_ref, g_ref, o_ref):
    m = (ga_ref[...].astype(F32) * _dot(oa_ref[...], wa_ref[...])
         + gb_ref[...].astype(F32) * _dot(ob_ref[...], wb_ref[...]))
    o_ref[...] = h_ref[...] + _rms(_dot(m.astype(BF16), wo_ref[...]), g_ref[...])


def _merge(h, oa, ob, ga, gb, w_proj_a, w_proj_b, w_out, post_g):
    n, d = h.shape
    tm = _pick_tile(n, MERGE_ROWS)

    def row(width):
        return pl.BlockSpec((tm, width), lambda i: (i, 0))

    return pl.pallas_call(
        _merge_kernel,
        grid=(n // tm,),
        in_specs=[row(d), row(oa.shape[1]), row(ob.shape[1]), row(d), row(d),
                  _resident(w_proj_a.shape), _resident(w_proj_b.shape), _resident(w_out.shape), _resident((1, d))],
        out_specs=row(d),
        out_shape=jax.ShapeDtypeStruct((n, d), F32),
        compiler_params=_cparams(("parallel",)),
        name="merge",
    )(h, oa, ob, ga, gb, w_proj_a, w_proj_b, w_out, post_g)


def kernel(x_prompt, x_sample, cache_k, cache_v, state_s, page_table, ffn1_pre_g, ffn1_post_g, ffn1_w_gu, ffn1_w_down, mix_pre_g, mix_post_g, w_in, lambda_q1, lambda_k1, lambda_q2, lambda_k2, attn_subln_g, hgrn_lb_logits, hgrn_norm_g, w_proj_a, w_proj_b, w_out, ffn2_pre_g, ffn2_post_g, ffn2_w_gu, ffn2_w_down):
    depth, d_model = ffn1_pre_g.shape
    bp, lp, _ = x_prompt.shape
    bs, ls, _ = x_sample.shape
    n_heads_a, two_dqk = cache_k.shape[3], cache_k.shape[4]
    d_qk = two_dqk // 2
    d_v_a = cache_v.shape[4]
    n_heads_b, d_k_b, d_v_b = state_s.shape[2:]
    w_qk, w_a = n_heads_a * two_dqk, n_heads_a * d_v_a
    w_kb, w_b = n_heads_b * d_k_b, n_heads_b * d_v_b
    widths = (w_qk, w_qk, w_a, w_kb, w_kb, w_b, w_b, d_model, d_model)
    assert sum(widths) == w_in.shape[2] and two_dqk == d_v_a == LANES and cache_k.shape[2] == PAGE_SIZE
    n_pool = cache_k.shape[1]

    yp = x_prompt.reshape(bp * lp, d_model)
    ys = x_sample.reshape(bs * ls, d_model)
    kp_l, vp_l, sp_l, ks_l, vs_l, ss_l = [], [], [], [], [], []
    for l in range(depth):
        lam_init = 0.8 - 0.6 * math.exp(-0.3 * l)
        row = lambda g: g[l].reshape(1, -1)
        wgu1, wdn1 = ffn1_w_gu[l].astype(BF16), ffn1_w_down[l].astype(BF16)
        wgu2, wdn2 = ffn2_w_gu[l].astype(BF16), ffn2_w_down[l].astype(BF16)
        win = w_in[l].astype(BF16)
        wpa, wpb, wo = w_proj_a[l].astype(BF16), w_proj_b[l].astype(BF16), w_out[l].astype(BF16)
        lam_params = jnp.stack([lambda_q1[l], lambda_k1[l], lambda_q2[l], lambda_k2[l]])
        subln_g, norm_g = row(attn_subln_g), row(hgrn_norm_g)
        ck = cache_k[l].reshape(n_pool, PAGE_SIZE, w_qk)
        cv = cache_v[l].reshape(n_pool, PAGE_SIZE, w_a)

        def mixer_inputs(y):
            h = _ffn(y, row(ffn1_pre_g), row(ffn1_post_g), wgu1, wdn1)
            return h, _proj(h, row(mix_pre_g), win, hgrn_lb_logits, widths, l, d_qk ** -0.5)

        def mixer_outputs(h, oa, ob, ga, gb):
            h2 = _merge(h, oa, ob, ga, gb, wpa, wpb, wo, row(mix_post_g))
            return _ffn(h2, row(ffn2_pre_g), row(ffn2_post_g), wgu2, wdn2)

        h, (qa, ka, ka16, va, va16, qb, lf, kb, vb, sg, ga, gb) = mixer_inputs(yp)
        seq = lambda a: a.reshape(bp, lp, a.shape[1])
        oa = _attn_prompt(seq(qa), seq(ka16), seq(va16), lam_params, subln_g,
                          n_heads=n_heads_a, d_qk=d_qk, lam_init=lam_init)
        ob, sp = _hgrn_prompt(seq(qb), seq(kb), seq(vb), seq(lf), seq(sg), norm_g, n_heads=n_heads_b)
        yp = mixer_outputs(h, oa.reshape(bp * lp, w_a), ob.reshape(bp * lp, w_b), ga, gb)
        kp_l.append(ka.reshape(bp, lp, n_heads_a, two_dqk))
        vp_l.append(va.reshape(bp, lp, n_heads_a, d_v_a))
        sp_l.append(sp)

        h, (qa, ka, ka16, va, va16, qb, lf, kb, vb, sg, ga, gb) = mixer_inputs(ys)
        seq = lambda a: a.reshape(bs, ls, a.shape[1])
        oa = _attn_decode(seq(qa), seq(ka), seq(va), ck, cv, page_table, lam_params, subln_g,
                          n_heads=n_heads_a, d_qk=d_qk, lam_init=lam_init)
        ob, ss = _hgrn_decode(seq(qb), seq(kb), seq(vb), seq(lf), seq(sg), norm_g, state_s[l], n_heads=n_heads_b)
        ys = mixer_outputs(h, oa.reshape(bs * ls, w_a), ob.reshape(bs * ls, w_b), ga, gb)
        ks_l.append(ka.reshape(bs, ls, n_heads_a, two_dqk))
        vs_l.append(va.reshape(bs, ls, n_heads_a, d_v_a))
        ss_l.append(ss)

    def layers(xs):
        return xs[0][None] if len(xs) == 1 else jnp.stack(xs)

    return (yp.reshape(bp, lp, d_model), ys.reshape(bs, ls, d_model),
            layers(kp_l), layers(vp_l), layers(sp_l), layers(ks_l), layers(vs_l), layers(ss_l))
```

```python
import functools
import math

import numpy as np
import jax
import jax.numpy as jnp
from jax import lax
from jax.experimental import pallas as pl
from jax.experimental.pallas import tpu as pltpu

F32 = jnp.float32
BF16 = jnp.bfloat16
EPS = 1e-6
NEG_INF = -1e30
LOG2E = math.log2(math.e)
PAGE_SIZE = 128
LANES = 128
SUB_CHUNK = 16
EXP_CLAMP = 80.0
VMEM_LIMIT = 56 * 1024 * 1024
FFN_ROWS = 1024
PROJ_ROWS = 512
MERGE_ROWS = 1024
ATTN_Q_ROWS = 512
ATTN_KV_ROWS = 512
ATTN_ROW_CHUNK = 512
HGRN_CHUNK = 128
DECODE_PAGES = 16


def _cparams(sem):
    return pltpu.CompilerParams(dimension_semantics=sem, vmem_limit_bytes=VMEM_LIMIT)


def _resident(shape):
    nd = len(shape)
    return pl.BlockSpec(shape, lambda *_: (0,) * nd, pipeline_mode=pl.Buffered(1))


def _rms(x, g):
    return x * lax.rsqrt(jnp.mean(x * x, axis=-1, keepdims=True) + EPS) * g


def _dot(a, b):
    return jnp.dot(a, b, preferred_element_type=F32)


def _dot_nt(a, b):
    return lax.dot_general(a, b, (((1,), (1,)), ((), ())), preferred_element_type=F32)


def _dot_tn(a, b):
    return lax.dot_general(a, b, (((0,), (0,)), ((), ())), preferred_element_type=F32)


def _pick_tile(n, target):
    t = math.gcd(n, target)
    assert t % 8 == 0 or t == n, (n, target)
    return t


def _ffn_kernel(x_ref, pre_ref, post_ref, wgu_ref, wdn_ref, o_ref, *, d_ff, ck):
    x = x_ref[...]
    xn = _rms(x, pre_ref[...]).astype(BF16)
    acc = None
    for c in range(d_ff // ck):
        g = _dot(xn, wgu_ref[:, c * ck:(c + 1) * ck])
        u = _dot(xn, wgu_ref[:, d_ff + c * ck:d_ff + (c + 1) * ck])
        a = (g * jax.nn.sigmoid(g) * u).astype(BF16)
        part = _dot(a, wdn_ref[c * ck:(c + 1) * ck, :])
        acc = part if acc is None else acc + part
    o_ref[...] = x + 0.5 * _rms(acc, post_ref[...])


def _ffn(x, pre_g, post_g, w_gu, w_down):
    n, d = x.shape
    d_ff = w_down.shape[0]
    tm = _pick_tile(n, FFN_ROWS)
    ck = 2 * LANES if d_ff % (2 * LANES) == 0 else d_ff
    row = pl.BlockSpec((tm, d), lambda i: (i, 0))
    return pl.pallas_call(
        functools.partial(_ffn_kernel, d_ff=d_ff, ck=ck),
        grid=(n // tm,),
        in_specs=[row, _resident((1, d)), _resident((1, d)), _resident(w_gu.shape), _resident(w_down.shape)],
        out_specs=row,
        out_shape=jax.ShapeDtypeStruct((n, d), F32),
        compiler_params=_cparams(("parallel",)),
        name="ffn",
    )(x, pre_g, post_g, w_gu, w_down)


def _proj_kernel(h_ref, g_ref, win_ref, lbl_ref,
                 qa_ref, ka_ref, ka16_ref, va_ref, va16_ref,
                 qb_ref, lf_ref, kb_ref, vb_ref, sg_ref, ga_ref, gb_ref, *, offs, layer, q_scale):
    u = _rms(h_ref[...], g_ref[...]).astype(BF16)

    def z(i):
        return _dot(u, win_ref[:, offs[i]:offs[i + 1]])

    def store_heads(ref, x):
        dh = ref.shape[2]
        for hh in range(ref.shape[1]):
            ref[:, hh, :] = x[:, hh * dh:(hh + 1) * dh]

    qa_ref[...] = (z(0) * q_scale).astype(BF16)
    k = z(1)
    store_heads(ka_ref, k)
    ka16_ref[...] = k.astype(BF16)
    v = z(2)
    store_heads(va_ref, v)
    va16_ref[...] = v.astype(BF16)
    qb = z(3)
    qb_ref[...] = (qb * jax.nn.sigmoid(qb)).astype(BF16)
    lbl = lbl_ref[...]
    e = jnp.exp(lbl - jnp.max(lbl, axis=0, keepdims=True))
    lb = jnp.sum(e[:layer + 1], axis=0, keepdims=True) / jnp.sum(e, axis=0, keepdims=True)
    fr = z(4)
    lf_ref[...] = jnp.log(lb + (1.0 - lb) * jax.nn.sigmoid(fr))
    kb_ref[...] = ((1.0 - lb) * jax.nn.sigmoid(-fr)).astype(BF16)
    vb_ref[...] = z(5).astype(BF16)
    gbr = z(6)
    sg_ref[...] = (gbr * jax.nn.sigmoid(gbr)).astype(BF16)
    ga_ref[...] = jax.nn.sigmoid(z(7)).astype(BF16)
    gb_ref[...] = jax.nn.sigmoid(z(8)).astype(BF16)


def _proj(h, mix_pre_g, w_in, lb_logits, widths, layer, q_scale, n_heads_a):
    n, d = h.shape
    tm = _pick_tile(n, PROJ_ROWS)
    offs = tuple(int(o) for o in np.concatenate([[0], np.cumsum(widths)]))
    w = widths

    def out(width, dtype):
        return pl.BlockSpec((tm, width), lambda i: (i, 0)), jax.ShapeDtypeStruct((n, width), dtype)

    def out_heads(width):
        shape = (n_heads_a, width // n_heads_a)
        return pl.BlockSpec((tm,) + shape, lambda i: (i, 0, 0)), jax.ShapeDtypeStruct((n,) + shape, F32)

    outs = [out(w[0], BF16), out_heads(w[1]), out(w[1], BF16), out_heads(w[2]), out(w[2], BF16),
            out(w[3], BF16), out(w[4], F32), out(w[4], BF16), out(w[5], BF16), out(w[6], BF16),
            out(w[7], BF16), out(w[8], BF16)]
    return pl.pallas_call(
        functools.partial(_proj_kernel, offs=offs, layer=layer, q_scale=q_scale),
        grid=(n // tm,),
        in_specs=[pl.BlockSpec((tm, d), lambda i: (i, 0)), _resident((1, d)), _resident(w_in.shape),
                  _resident(lb_logits.shape)],
        out_specs=[o[0] for o in outs],
        out_shape=[o[1] for o in outs],
        compiler_params=_cparams(("parallel",)),
        name="proj",
    )(h, mix_pre_g, w_in, lb_logits)


def _alibi_slope(h, n_heads):
    slope = jnp.float32(0.0)
    for hh in range(n_heads):
        slope = jnp.where(h == hh, jnp.float32(2.0 ** (-8.0 * (hh + 1) / n_heads)), slope)
    return slope


def _lambda(lam_ref, lam_init):
    lp = lam_ref[...]
    l1 = jnp.exp(jnp.sum(lp[0:1] * lp[1:2], axis=-1, keepdims=True))
    l2 = jnp.exp(jnp.sum(lp[2:3] * lp[3:4], axis=-1, keepdims=True))
    return l1 - l2 + lam_init


def _split_maps(x, d_qk):
    lane = lax.broadcasted_iota(jnp.int32, x.shape, 1)
    zero = jnp.zeros_like(x)
    return jnp.where(lane < d_qk, x, zero), jnp.where(lane >= d_qk, x, zero)


def _softmax_update(s, v, m_prev, l_prev, acc_prev):
    m_new = jnp.maximum(m_prev, jnp.max(s, axis=-1, keepdims=True))
    alpha = jnp.exp2(m_prev - m_new)
    p = jnp.exp2(s - m_new)
    l_new = alpha * l_prev + jnp.sum(p, axis=-1, keepdims=True)
    acc_new = alpha * acc_prev + _dot(p.astype(BF16), v)
    return m_new, l_new, acc_new


def _diff_finish(acc, l, lam, g, out_scale, rows):
    o = acc[:rows] / l[:rows] - lam * (acc[rows:] / l[rows:])
    return _rms(o, g) * out_scale


def _attn_kernel(qi_ref, kj_ref, qn_ref, kn_ref, q0_ref, k0_ref, qx_ref, kx_ref, v_ref, lam_ref, g_ref,
                 o_ref, s_sc, m_sc, l_sc, acc_sc, *, tq, tk, rc, d_qk, n_heads, lam_init):
    h = pl.program_id(1)
    t = pl.program_id(2)
    i, j = qi_ref[t], kj_ref[t]
    i_n, j_n = qn_ref[t], kn_ref[t]
    last_j = ((i + 1) * tq - 1) // tk
    slope = _alibi_slope(h, n_heads) * LOG2E
    lane_pos = lax.broadcasted_iota(jnp.int32, (1, tk), 1)

    def scores(q_ref, kmaps, i_s, j_s, mp, r0, masked):
        kpos = j_s * tk + lane_pos
        s = _dot_nt(q_ref[0, r0:r0 + rc, :], kmaps[mp]) + slope * (kpos - i_s * tq).astype(F32)
        if masked:
            qpos = i_s * tq + r0 + lax.broadcasted_iota(jnp.int32, (rc, 1), 0)
            s = jnp.where(kpos <= qpos, s, NEG_INF)
        return s

    chunks = [(mp, r0) for mp in range(2) for r0 in range(0, tq, rc)]

    @pl.when(t == 0)
    def _():
        kmaps = _split_maps(k0_ref[0], d_qk)
        for mp, r0 in chunks:
            s_sc[pl.ds(mp * tq + r0, rc), :] = scores(q0_ref, kmaps, 0, 0, mp, r0, True)

    @pl.when(j == 0)
    def _():
        m_sc[...] = jnp.full(m_sc.shape, NEG_INF, F32)
        l_sc[...] = jnp.zeros(l_sc.shape, F32)
        acc_sc[...] = jnp.zeros(acc_sc.shape, F32)

    def step(next_masked):
        kmaps_n = _split_maps(kx_ref[0], d_qk)
        v = v_ref[0]
        for mp, r0 in chunks:
            rows = pl.ds(mp * tq + r0, rc)
            s = s_sc[rows, :]
            s_sc[rows, :] = scores(qx_ref, kmaps_n, i_n, j_n, mp, r0, next_masked)
            tiles = [s[:, c * LANES:(c + 1) * LANES] for c in range(tk // LANES)]
            m_cur = jnp.max(functools.reduce(jnp.maximum, tiles), axis=-1, keepdims=True)
            m_prev = m_sc[rows, :]
            m_new = jnp.maximum(m_prev, m_cur)
            alpha = jnp.exp2(m_prev - m_new)
            p_tiles = [jnp.exp2(tl - m_new) for tl in tiles]
            l_sc[rows, :] = alpha * l_sc[rows, :] + functools.reduce(jnp.add, p_tiles)
            p = jnp.concatenate([pt.astype(BF16) for pt in p_tiles], axis=1)
            acc_sc[rows, :] = alpha * acc_sc[rows, :] + _dot(p, v)
            m_sc[rows, :] = m_new

    next_masked = (j_n + 1) * tk - 1 > i_n * tq

    @pl.when(next_masked)
    def _():
        step(True)

    @pl.when(jnp.logical_not(next_masked))
    def _():
        step(False)

    @pl.when(j == last_j)
    def _():
        l = jnp.sum(l_sc[...], axis=-1, keepdims=True)
        o = _diff_finish(acc_sc[...], l, _lambda(lam_ref, lam_init), g_ref[...], 1.0 - lam_init, tq)
        o_ref[0] = o.astype(o_ref.dtype)


def _attn_prompt(qa, ka, va, lam_params, subln_g, *, n_heads, d_qk, lam_init):
    b, l, w = qa.shape
    dh = w // n_heads
    tq = _pick_tile(l, ATTN_Q_ROWS)
    tk = _pick_tile(l, ATTN_KV_ROWS)
    pairs = [(i, j) for i in range(l // tq) for j in range(((i + 1) * tq - 1) // tk + 1)]
    nxt = pairs[1:] + pairs[-1:]
    tables = [jnp.asarray(col, jnp.int32) for col in
              ([p[0] for p in pairs], [p[1] for p in pairs], [p[0] for p in nxt], [p[1] for p in nxt])]

    def blk(rows, index):
        return pl.BlockSpec((1, rows, dh), index)

    grid_spec = pltpu.PrefetchScalarGridSpec(
        num_scalar_prefetch=4,
        grid=(b, n_heads, len(pairs)),
        in_specs=[
            blk(tq, lambda bb, hh, t, qi, kj, qn, kn: (bb, 0, hh)),
            blk(tk, lambda bb, hh, t, qi, kj, qn, kn: (bb, 0, hh)),
            blk(tq, lambda bb, hh, t, qi, kj, qn, kn: (bb, qn[t], hh)),
            blk(tk, lambda bb, hh, t, qi, kj, qn, kn: (bb, kn[t], hh)),
            blk(tk, lambda bb, hh, t, qi, kj, qn, kn: (bb, kj[t], hh)),
            pl.BlockSpec(lam_params.shape, lambda bb, hh, t, qi, kj, qn, kn: (0, 0)),
            pl.BlockSpec(subln_g.shape, lambda bb, hh, t, qi, kj, qn, kn: (0, 0)),
        ],
        out_specs=blk(tq, lambda bb, hh, t, qi, kj, qn, kn: (bb, qi[t], hh)),
        scratch_shapes=[pltpu.VMEM((2 * tq, tk), F32), pltpu.VMEM((2 * tq, LANES), F32),
                        pltpu.VMEM((2 * tq, LANES), F32), pltpu.VMEM((2 * tq, dh), F32)],
    )
    rc = _pick_tile(tq, ATTN_ROW_CHUNK)
    return pl.pallas_call(
        functools.partial(_attn_kernel, tq=tq, tk=tk, rc=rc, d_qk=d_qk, n_heads=n_heads, lam_init=lam_init),
        grid_spec=grid_spec,
        out_shape=jax.ShapeDtypeStruct((b, l, w), BF16),
        compiler_params=_cparams(("parallel", "parallel", "arbitrary")),
        name="attn_prompt",
    )(*tables, qa, ka, qa, ka, va, lam_params, subln_g)


def _attn_decode_kernel(pt_ref, q_ref, kn_ref, vn_ref, lam_ref, g_ref, *rest,
                        pp, n_heads, d_qk, lam_init, past_len):
    k_refs = rest[:pp]
    v_refs = rest[pp:2 * pp]
    o_ref, m_sc, l_sc, acc_sc = rest[2 * pp:]
    c = pl.program_id(1)
    nq = q_ref.shape[1]
    dh = q_ref.shape[2] // n_heads
    page = k_refs[0].shape[0] // n_heads

    def head_rows(ref, h):
        return ref[pl.ds(h, page, stride=n_heads), :].astype(BF16)

    @pl.when(c == 0)
    def _():
        m_sc[...] = jnp.full(m_sc.shape, NEG_INF, F32)
        l_sc[...] = jnp.zeros(l_sc.shape, F32)
        acc_sc[...] = jnp.zeros(acc_sc.shape, F32)

    def q_rows(h):
        qh = q_ref[0][:, h * dh:(h + 1) * dh].astype(F32)
        q1, q2 = _split_maps(qh, d_qk)
        return jnp.concatenate([q1, q2], axis=0).astype(BF16)

    heads = range(n_heads)
    slopes = [jnp.float32(2.0 ** (-8.0 * (h + 1) / n_heads) * LOG2E) for h in heads]
    qh = [q_rows(h) for h in heads]
    rel = (c * (pp * page) - past_len + lax.broadcasted_iota(jnp.int32, (1, pp * page), 1)).astype(F32)
    s = [jnp.concatenate([_dot_nt(qh[h], head_rows(k_refs[p], h)) for p in range(pp)], axis=1) + slopes[h] * rel
         for h in heads]
    m_prev = [m_sc[h] for h in heads]
    m_new = [jnp.maximum(m_prev[h], jnp.max(s[h], axis=-1, keepdims=True)) for h in heads]
    alpha = [jnp.exp2(m_prev[h] - m_new[h]) for h in heads]
    p_all = [jnp.exp2(s[h] - m_new[h]) for h in heads]
    p16 = [p.astype(BF16) for p in p_all]
    pv = [functools.reduce(jnp.add, [_dot(p16[h][:, p * page:(p + 1) * page], head_rows(v_refs[p], h))
                                     for p in range(pp)]) for h in heads]
    for h in heads:
        l_sc[h] = alpha[h] * l_sc[h] + jnp.sum(p_all[h], axis=-1, keepdims=True)
        acc_sc[h] = alpha[h] * acc_sc[h] + pv[h]
        m_sc[h] = m_new[h]

    @pl.when(c == pl.num_programs(1) - 1)
    def _():
        lam = _lambda(lam_ref, lam_init)
        outs = []
        for h in range(n_heads):
            slope = jnp.float32(2.0 ** (-8.0 * (h + 1) / n_heads) * LOG2E)
            pad = jnp.zeros((page - nq, dh), F32)
            kn = jnp.concatenate([kn_ref[:, h, :], pad], axis=0).astype(BF16)
            vn = jnp.concatenate([vn_ref[:, h, :], pad], axis=0).astype(BF16)
            s = _dot_nt(q_rows(h), kn)
            jj = lax.broadcasted_iota(jnp.int32, (1, page), 1)
            r = lax.broadcasted_iota(jnp.int32, (2 * nq, 1), 0)
            ii = jnp.where(r >= nq, r - nq, r)
            s = jnp.where(jj <= ii, s + slope * jj.astype(F32), NEG_INF)
            _, l_new, acc_new = _softmax_update(s, vn, m_sc[h], l_sc[h], acc_sc[h])
            outs.append(_diff_finish(acc_new, l_new, lam, g_ref[...], 1.0 - lam_init, nq))
        o_ref[0] = jnp.concatenate(outs, axis=1).astype(o_ref.dtype)


def _attn_decode(qa, k_new, v_new, cache_k, cache_v, layer, page_table, lam_params, subln_g,
                 *, n_heads, d_qk, lam_init):
    b, nq, w = qa.shape
    n_pages = page_table.shape[1]
    pp = math.gcd(n_pages, DECODE_PAGES)
    dh = cache_k.shape[3]
    page = cache_k.shape[2] // n_heads

    def page_spec(p):
        return pl.BlockSpec((None, None, page * n_heads, dh),
                            lambda bb, c, pt: (layer, pt[bb * n_pages + c * pp + p], 0, 0))

    def tok_spec():
        return pl.BlockSpec((1, nq, w), lambda bb, c, pt: (bb, 0, 0))

    def new_spec():
        return pl.BlockSpec((None, nq, n_heads, dh), lambda bb, c, pt: (bb, 0, 0, 0))

    grid_spec = pltpu.PrefetchScalarGridSpec(
        num_scalar_prefetch=1,
        grid=(b, n_pages // pp),
        in_specs=[tok_spec(), new_spec(), new_spec(),
                  pl.BlockSpec(lam_params.shape, lambda bb, c, pt: (0, 0)),
                  pl.BlockSpec(subln_g.shape, lambda bb, c, pt: (0, 0))]
                 + [page_spec(p) for p in range(pp)] + [page_spec(p) for p in range(pp)],
        out_specs=tok_spec(),
        scratch_shapes=[pltpu.VMEM((n_heads, 2 * nq, 1), F32), pltpu.VMEM((n_heads, 2 * nq, 1), F32),
                        pltpu.VMEM((n_heads, 2 * nq, w // n_heads), F32)],
    )
    return pl.pallas_call(
        functools.partial(_attn_decode_kernel, pp=pp, n_heads=n_heads, d_qk=d_qk, lam_init=lam_init,
                          past_len=n_pages * page),
        grid_spec=grid_spec,
        out_shape=jax.ShapeDtypeStruct((b, nq, w), BF16),
        compiler_params=_cparams(("parallel", "arbitrary")),
        name="attn_decode",
    )(page_table.reshape(-1), qa, k_new, v_new, lam_params, subln_g,
      *([cache_k] * pp), *([cache_v] * pp))


def _cumsum_rows(x):
    c = x.shape[0]
    tri = (lax.broadcasted_iota(jnp.int32, (c, c), 0) >= lax.broadcasted_iota(jnp.int32, (c, c), 1))
    tri = jnp.where(tri, 1.0, 0.0).astype(BF16)
    hi = x.astype(BF16)
    r1 = x - hi.astype(F32)
    mid = r1.astype(BF16)
    lo = (r1 - mid.astype(F32)).astype(BF16)
    return _dot(tri, hi) + _dot(tri, mid) + _dot(tri, lo)


def _hgrn_kernel(q_ref, k_ref, v_ref, lf_ref, sg_ref, g_ref, o_ref, s_ref, st_sc, *, n_heads):
    c_idx = pl.program_id(1)
    c = q_ref.shape[1]
    dk = q_ref.shape[2] // n_heads
    dv = v_ref.shape[2] // n_heads
    t_sub = min(SUB_CHUNK, c)
    n_sub = c // t_sub

    @pl.when(c_idx == 0)
    def _():
        st_sc[...] = jnp.zeros(st_sc.shape, F32)

    q = q_ref[0].astype(F32)
    k = k_ref[0].astype(F32)
    v16 = v_ref[0]
    b = _cumsum_rows(lf_ref[0])
    anchors = [b[i * t_sub + t_sub // 2 - 1:i * t_sub + t_sub // 2, :] for i in range(n_sub)]
    anchor_rows = jnp.concatenate([jnp.broadcast_to(a, (t_sub, a.shape[1])) for a in anchors], axis=0)
    q_intra = (q * jnp.exp(jnp.minimum(b - anchor_rows, EXP_CLAMP))).astype(BF16)
    q_inter = (q * jnp.exp(b)).astype(BF16)
    b_last = b[c - 1:c, :]
    k_state = (k * jnp.exp(b_last - b)).astype(BF16)
    decay = jnp.exp(b_last)
    k_parts = []
    for i, a in enumerate(anchors):
        hi = (i + 1) * t_sub
        k_parts.append((k[:hi] * jnp.exp(jnp.minimum(a - b[:hi], EXP_CLAMP))).astype(BF16))
        if hi < c:
            k_parts.append(jnp.zeros((c - hi, k.shape[1]), BF16))
    k_intra = jnp.concatenate(k_parts, axis=0)

    heads = range(n_heads)
    ks = [slice(h * dk, (h + 1) * dk) for h in heads]
    vs = [slice(h * dv, (h + 1) * dv) for h in heads]
    tt = lax.broadcasted_iota(jnp.int32, (c, c), 0)
    ss = lax.broadcasted_iota(jnp.int32, (c, c), 1)
    wide = [_dot_nt(q_intra[:, ks[h]], k_intra[:, ks[h]]) for h in heads]
    a_mat = [jnp.concatenate([w[i * t_sub:(i + 1) * t_sub, i * c:(i + 1) * c] for i in range(n_sub)], axis=0)
             for w in wide]
    a_mat = [jnp.where(ss <= tt, a, 0.0).astype(BF16) for a in a_mat]
    st = [st_sc[h] for h in heads]
    o = [_dot_nt(q_inter[:, ks[h]], st[h].astype(BF16)) + _dot(a_mat[h], v16[:, vs[h]]) for h in heads]
    for h in heads:
        st_sc[h] = decay[:, ks[h]] * st[h] + _dot_tn(v16[:, vs[h]], k_state[:, ks[h]])
    outs = [_rms(o[h], g_ref[...]) * sg_ref[0][:, vs[h]].astype(F32) for h in heads]
    o_ref[0] = jnp.concatenate(outs, axis=1).astype(o_ref.dtype)

    @pl.when(c_idx == pl.num_programs(1) - 1)
    def _():
        for h in range(n_heads):
            s_ref[0, h] = st_sc[h].T


def _hgrn_prompt(qb, kb, vb, lf, sg, norm_g, *, n_heads):
    b, l, wk = qb.shape
    wv = vb.shape[2]
    c = _pick_tile(l, HGRN_CHUNK)
    dk, dv = wk // n_heads, wv // n_heads

    def tok(width):
        return pl.BlockSpec((1, c, width), lambda bb, cc: (bb, cc, 0))

    return pl.pallas_call(
        functools.partial(_hgrn_kernel, n_heads=n_heads),
        grid=(b, l // c),
        in_specs=[tok(wk), tok(wk), tok(wv), tok(wk), tok(wv), pl.BlockSpec(norm_g.shape, lambda bb, cc: (0, 0))],
        out_specs=[tok(wv), pl.BlockSpec((1, n_heads, dk, dv), lambda bb, cc: (bb, 0, 0, 0))],
        out_shape=[jax.ShapeDtypeStruct((b, l, wv), BF16), jax.ShapeDtypeStruct((b, n_heads, dk, dv), F32)],
        scratch_shapes=[pltpu.VMEM((n_heads, dv, dk), F32)],
        compiler_params=_cparams(("parallel", "arbitrary")),
        name="hgrn_prompt",
    )(qb, kb, vb, lf, sg, norm_g)


def _hgrn_decode_kernel(q_ref, k_ref, v_ref, lf_ref, sg_ref, g_ref, s0_ref, o_ref, s_ref, *, n_heads):
    n = q_ref.shape[1]
    dk = q_ref.shape[2] // n_heads
    dv = v_ref.shape[2] // n_heads
    row = lax.broadcasted_iota(jnp.int32, (n, 1), 0)
    pad = 2 * 8 - n if n < 2 * 8 else 0
    outs = []
    for h in range(n_heads):
        ks = slice(h * dk, (h + 1) * dk)
        vs = slice(h * dv, (h + 1) * dv)
        q = q_ref[0][:, ks].astype(F32)
        k = k_ref[0][:, ks].astype(F32)
        v = v_ref[0][:, vs].astype(F32)
        lf = lf_ref[0][:, ks]
        st = s0_ref[0, h].T
        b = lf
        for sh in [1 << e for e in range((n - 1).bit_length())]:
            shifted = jnp.concatenate([jnp.zeros((sh, dk), F32), b[:n - sh]], axis=0)
            b = b + shifted

        def padded(x):
            return jnp.concatenate([x, jnp.zeros((pad, x.shape[1]), F32)], axis=0) if pad else x

        o = _dot_nt(padded(q * jnp.exp(b)).astype(BF16), st.astype(BF16))[:n]
        for s in range(n):
            w = jnp.sum(q * k[s:s + 1] * jnp.exp(jnp.minimum(b - b[s:s + 1], 0.0)), axis=-1, keepdims=True)
            o = o + jnp.where(row >= s, w, 0.0) * v[s:s + 1]
        b_last = b[n - 1:n]
        kd = padded(k * jnp.exp(b_last - b)).astype(BF16)
        st_new = jnp.exp(b_last) * st + _dot_tn(padded(v).astype(BF16), kd)
        s_ref[0, h] = st_new.T
        outs.append(_rms(o, g_ref[...]) * sg_ref[0][:, vs].astype(F32))
    o_ref[0] = jnp.concatenate(outs, axis=1).astype(o_ref.dtype)


def _hgrn_decode(qb, kb, vb, lf, sg, norm_g, state, layer, *, n_heads):
    b, n, wk = qb.shape
    wv = vb.shape[2]
    dk, dv = wk // n_heads, wv // n_heads

    def tok(width):
        return pl.BlockSpec((1, n, width), lambda bb: (bb, 0, 0))

    st = pl.BlockSpec((1, n_heads, dk, dv), lambda bb: (bb, 0, 0, 0))
    st_in = pl.BlockSpec((None, 1, n_heads, dk, dv), lambda bb: (layer, bb, 0, 0, 0))
    return pl.pallas_call(
        functools.partial(_hgrn_decode_kernel, n_heads=n_heads),
        grid=(b,),
        in_specs=[tok(wk), tok(wk), tok(wv), tok(wk), tok(wv), pl.BlockSpec(norm_g.shape, lambda bb: (0, 0)), st_in],
        out_specs=[tok(wv), st],
        out_shape=[jax.ShapeDtypeStruct((b, n, wv), BF16), jax.ShapeDtypeStruct((b, n_heads, dk, dv), F32)],
        compiler_params=_cparams(("parallel",)),
        name="hgrn_decode",
    )(qb, kb, vb, lf, sg, norm_g, state)


def _merge_kernel(h_ref, oa_ref, ob_ref, ga_ref, gb_ref, wa_ref, wb_ref, wo_ref, g_ref, o_ref):
    m = (ga_ref[...].astype(F32) * _dot(oa_ref[...], wa_ref[...])
         + gb_ref[...].astype(F32) * _dot(ob_ref[...], wb_ref[...]))
    o_ref[...] = h_ref[...] + _rms(_dot(m.astype(BF16), wo_ref[...]), g_ref[...])


def _merge(h, oa, ob, ga, gb, w_proj_a, w_proj_b, w_out, post_g):
    n, d = h.shape
    tm = _pick_tile(n, MERGE_ROWS)

    def row(width):
        return pl.BlockSpec((tm, width), lambda i: (i, 0))

    return pl.pallas_call(
        _merge_kernel,
        grid=(n // tm,),
        in_specs=[row(d), row(oa.shape[1]), row(ob.shape[1]), row(d), row(d),
                  _resident(w_proj_a.shape), _resident(w_proj_b.shape), _resident(w_out.shape), _resident((1, d))],
        out_specs=row(d),
        out_shape=jax.ShapeDtypeStruct((n, d), F32),
        compiler_params=_cparams(("parallel",)),
        name="merge",
    )(h, oa, ob, ga, gb, w_proj_a, w_proj_b, w_out, post_g)


def kernel(x_prompt, x_sample, cache_k, cache_v, state_s, page_table, ffn1_pre_g, ffn1_post_g, ffn1_w_gu, ffn1_w_down, mix_pre_g, mix_post_g, w_in, lambda_q1, lambda_k1, lambda_q2, lambda_k2, attn_subln_g, hgrn_lb_logits, hgrn_norm_g, w_proj_a, w_proj_b, w_out, ffn2_pre_g, ffn2_post_g, ffn2_w_gu, ffn2_w_down):
    depth, d_model = ffn1_pre_g.shape
    bp, lp, _ = x_prompt.shape
    bs, ls, _ = x_sample.shape
    n_heads_a, two_dqk = cache_k.shape[3], cache_k.shape[4]
    d_qk = two_dqk // 2
    d_v_a = cache_v.shape[4]
    n_heads_b, d_k_b, d_v_b = state_s.shape[2:]
    w_qk, w_a = n_heads_a * two_dqk, n_heads_a * d_v_a
    w_kb, w_b = n_heads_b * d_k_b, n_heads_b * d_v_b
    widths = (w_qk, w_qk, w_a, w_kb, w_kb, w_b, w_b, d_model, d_model)
    assert sum(widths) == w_in.shape[2] and two_dqk == d_v_a == LANES and cache_k.shape[2] == PAGE_SIZE

    cache_k2 = cache_k.reshape(depth, cache_k.shape[1], PAGE_SIZE * n_heads_a, two_dqk)
    cache_v2 = cache_v.reshape(depth, cache_v.shape[1], PAGE_SIZE * n_heads_a, d_v_a)
    yp = x_prompt.reshape(bp * lp, d_model)
    ys = x_sample.reshape(bs * ls, d_model)
    kp_l, vp_l, sp_l, ks_l, vs_l, ss_l = [], [], [], [], [], []
    for l in range(depth):
        lam_init = 0.8 - 0.6 * math.exp(-0.3 * l)
        row = lambda g: g[l].reshape(1, -1)
        wgu1, wdn1 = ffn1_w_gu[l].astype(BF16), ffn1_w_down[l].astype(BF16)
        wgu2, wdn2 = ffn2_w_gu[l].astype(BF16), ffn2_w_down[l].astype(BF16)
        win = w_in[l].astype(BF16)
        wpa, wpb, wo = w_proj_a[l].astype(BF16), w_proj_b[l].astype(BF16), w_out[l].astype(BF16)
        lam_params = jnp.stack([lambda_q1[l], lambda_k1[l], lambda_q2[l], lambda_k2[l]])
        subln_g, norm_g = row(attn_subln_g), row(hgrn_norm_g)

        def mixer_inputs(y):
            h = _ffn(y, row(ffn1_pre_g), row(ffn1_post_g), wgu1, wdn1)
            return h, _proj(h, row(mix_pre_g), win, hgrn_lb_logits, widths, l, d_qk ** -0.5 * LOG2E, n_heads_a)

        def mixer_outputs(h, oa, ob, ga, gb):
            h2 = _merge(h, oa, ob, ga, gb, wpa, wpb, wo, row(mix_post_g))
            return _ffn(h2, row(ffn2_pre_g), row(ffn2_post_g), wgu2, wdn2)

        h, (qa, ka, ka16, va, va16, qb, lf, kb, vb, sg, ga, gb) = mixer_inputs(yp)
        seq = lambda a: a.reshape(bp, lp, a.shape[1])
        oa = _attn_prompt(seq(qa), seq(ka16), seq(va16), lam_params, subln_g,
                          n_heads=n_heads_a, d_qk=d_qk, lam_init=lam_init)
        ob, sp = _hgrn_prompt(seq(qb), seq(kb), seq(vb), seq(lf), seq(sg), norm_g, n_heads=n_heads_b)
        yp = mixer_outputs(h, oa.reshape(bp * lp, w_a), ob.reshape(bp * lp, w_b), ga, gb)
        kp_l.append(ka.reshape(bp, lp, n_heads_a, two_dqk))
        vp_l.append(va.reshape(bp, lp, n_heads_a, d_v_a))
        sp_l.append(sp)

        h, (qa, ka, ka16, va, va16, qb, lf, kb, vb, sg, ga, gb) = mixer_inputs(ys)
        seq = lambda a: a.reshape(bs, ls, a.shape[1])
        ka_s = ka.reshape(bs, ls, n_heads_a, two_dqk)
        va_s = va.reshape(bs, ls, n_heads_a, d_v_a)
        oa = _attn_decode(seq(qa), ka_s, va_s, cache_k2, cache_v2, l, page_table, lam_params, subln_g,
                          n_heads=n_heads_a, d_qk=d_qk, lam_init=lam_init)
        ob, ss = _hgrn_decode(seq(qb), seq(kb), seq(vb), seq(lf), seq(sg), norm_g, state_s, l, n_heads=n_heads_b)
        ys = mixer_outputs(h, oa.reshape(bs * ls, w_a), ob.reshape(bs * ls, w_b), ga, gb)
        ks_l.append(ka_s)
        vs_l.append(va_s)
        ss_l.append(ss)

    def layers(xs):
        return xs[0][None] if len(xs) == 1 else jnp.stack(xs)

    return (yp.reshape(bp, lp, d_model), ys.reshape(bs, ls, d_model),
            layers(kp_l), layers(vp_l), layers(sp_l), layers(ks_l), layers(vs_l), layers(ss_l))
```

```python
import functools
import math

import numpy as np
import jax
import jax.numpy as jnp
from jax import lax
from jax.experimental import pallas as pl
from jax.experimental.pallas import tpu as pltpu

F32 = jnp.float32
BF16 = jnp.bfloat16
EPS = 1e-6
NEG_INF = -1e30
LOG2E = math.log2(math.e)
PAGE_SIZE = 128
LANES = 128
SUB_CHUNK = 16
EXP_CLAMP = 80.0
N_FEAT = 9
VMEM_LIMIT = 56 * 1024 * 1024
FFN_ROWS = 1024
PROJ_ROWS = 512
MERGE_ROWS = 1024
ATTN_Q_ROWS = 512
ATTN_KV_ROWS = 512
ATTN_ROW_CHUNK = 512
HGRN_CHUNK = 128
DECODE_PAGES = 32


def _cparams(sem):
    return pltpu.CompilerParams(dimension_semantics=sem, vmem_limit_bytes=VMEM_LIMIT)


def _resident(shape):
    nd = len(shape)
    return pl.BlockSpec(shape, lambda *_: (0,) * nd, pipeline_mode=pl.Buffered(1))


def _rms(x, g):
    return x * lax.rsqrt(jnp.mean(x * x, axis=-1, keepdims=True) + EPS) * g


def _dot(a, b):
    return jnp.dot(a, b, preferred_element_type=F32)


def _dot_nt(a, b):
    return lax.dot_general(a, b, (((1,), (1,)), ((), ())), preferred_element_type=F32)


def _dot_tn(a, b):
    return lax.dot_general(a, b, (((0,), (0,)), ((), ())), preferred_element_type=F32)


def _pick_tile(n, target):
    t = math.gcd(n, target)
    assert t % 8 == 0 or t == n, (n, target)
    return t


def _ffn_kernel(x_ref, pre_ref, post_ref, wgu_ref, wdn_ref, o_ref, *, d_ff, ck):
    x = x_ref[...]
    xn = _rms(x, pre_ref[...]).astype(BF16)
    acc = None
    for c in range(d_ff // ck):
        g = _dot(xn, wgu_ref[:, c * ck:(c + 1) * ck])
        u = _dot(xn, wgu_ref[:, d_ff + c * ck:d_ff + (c + 1) * ck])
        a = (g * jax.nn.sigmoid(g) * u).astype(BF16)
        part = _dot(a, wdn_ref[c * ck:(c + 1) * ck, :])
        acc = part if acc is None else acc + part
    o_ref[...] = x + 0.5 * _rms(acc, post_ref[...])


def _ffn(x, pre_g, post_g, w_gu, w_down):
    n, d = x.shape
    d_ff = w_down.shape[0]
    tm = _pick_tile(n, FFN_ROWS)
    ck = 2 * LANES if d_ff % (2 * LANES) == 0 else d_ff
    row = pl.BlockSpec((tm, d), lambda i: (i, 0))
    return pl.pallas_call(
        functools.partial(_ffn_kernel, d_ff=d_ff, ck=ck),
        grid=(n // tm,),
        in_specs=[row, _resident((1, d)), _resident((1, d)), _resident(w_gu.shape), _resident(w_down.shape)],
        out_specs=row,
        out_shape=jax.ShapeDtypeStruct((n, d), F32),
        compiler_params=_cparams(("parallel",)),
        name="ffn",
    )(x, pre_g, post_g, w_gu, w_down)


def _proj_kernel(h_ref, g_ref, win_ref, lbl_ref,
                 qa_ref, ka_ref, ka16_ref, va_ref, va16_ref,
                 qb_ref, lf_ref, kb_ref, vb_ref, sg_ref, ga_ref, gb_ref, *, offs, layer, q_scale):
    u = _rms(h_ref[...], g_ref[...]).astype(BF16)

    def z(i):
        return _dot(u, win_ref[:, offs[i]:offs[i + 1]])

    def store_heads(ref, x):
        dh = ref.shape[1]
        nh = x.shape[1] // dh
        for hh in range(nh):
            ref[pl.ds(hh, x.shape[0], stride=nh), :] = x[:, hh * dh:(hh + 1) * dh]

    qa_ref[...] = (z(0) * q_scale).astype(BF16)
    k = z(1)
    store_heads(ka_ref, k)
    ka16_ref[...] = k.astype(BF16)
    v = z(2)
    store_heads(va_ref, v)
    va16_ref[...] = v.astype(BF16)
    qb = z(3)
    qb_ref[...] = (qb * jax.nn.sigmoid(qb)).astype(BF16)
    lbl = lbl_ref[...]
    e = jnp.exp(lbl - jnp.max(lbl, axis=0, keepdims=True))
    lb = jnp.sum(e[:layer + 1], axis=0, keepdims=True) / jnp.sum(e, axis=0, keepdims=True)
    fr = z(4)
    lf_ref[...] = jnp.log(lb + (1.0 - lb) * jax.nn.sigmoid(fr))
    kb_ref[...] = ((1.0 - lb) * jax.nn.sigmoid(-fr)).astype(BF16)
    vb_ref[...] = z(5).astype(BF16)
    gbr = z(6)
    sg_ref[...] = (gbr * jax.nn.sigmoid(gbr)).astype(BF16)
    ga_ref[...] = jax.nn.sigmoid(z(7)).astype(BF16)
    gb_ref[...] = jax.nn.sigmoid(z(8)).astype(BF16)


def _proj(h, mix_pre_g, w_in, lb_logits, widths, layer, q_scale, n_heads_a):
    n, d = h.shape
    tm = _pick_tile(n, PROJ_ROWS)
    offs = tuple(int(o) for o in np.concatenate([[0], np.cumsum(widths)]))
    w = widths

    def out(width, dtype):
        return pl.BlockSpec((tm, width), lambda i: (i, 0)), jax.ShapeDtypeStruct((n, width), dtype)

    def out_heads(width):
        dh = width // n_heads_a
        return (pl.BlockSpec((tm * n_heads_a, dh), lambda i: (i, 0)),
                jax.ShapeDtypeStruct((n * n_heads_a, dh), F32))

    outs = [out(w[0], BF16), out_heads(w[1]), out(w[1], BF16), out_heads(w[2]), out(w[2], BF16),
            out(w[3], BF16), out(w[4], F32), out(w[4], BF16), out(w[5], BF16), out(w[6], BF16),
            out(w[7], BF16), out(w[8], BF16)]
    return pl.pallas_call(
        functools.partial(_proj_kernel, offs=offs, layer=layer, q_scale=q_scale),
        grid=(n // tm,),
        in_specs=[pl.BlockSpec((tm, d), lambda i: (i, 0)), _resident((1, d)), _resident(w_in.shape),
                  _resident(lb_logits.shape)],
        out_specs=[o[0] for o in outs],
        out_shape=[o[1] for o in outs],
        compiler_params=_cparams(("parallel",)),
        name="proj",
    )(h, mix_pre_g, w_in, lb_logits)


def _alibi_slope(h, n_heads):
    slope = jnp.float32(0.0)
    for hh in range(n_heads):
        slope = jnp.where(h == hh, jnp.float32(2.0 ** (-8.0 * (hh + 1) / n_heads)), slope)
    return slope


def _lambda(lam_ref, lam_init):
    lp = lam_ref[...]
    l1 = jnp.exp(jnp.sum(lp[0:1] * lp[1:2], axis=-1, keepdims=True))
    l2 = jnp.exp(jnp.sum(lp[2:3] * lp[3:4], axis=-1, keepdims=True))
    return l1 - l2 + lam_init


def _split_maps(x, d_qk):
    lane = lax.broadcasted_iota(jnp.int32, x.shape, 1)
    zero = jnp.zeros_like(x)
    return jnp.where(lane < d_qk, x, zero), jnp.where(lane >= d_qk, x, zero)


def _softmax_update(s, v, m_prev, l_prev, acc_prev):
    m_new = jnp.maximum(m_prev, jnp.max(s, axis=-1, keepdims=True))
    alpha = jnp.exp2(m_prev - m_new)
    p = jnp.exp2(s - m_new)
    l_new = alpha * l_prev + jnp.sum(p, axis=-1, keepdims=True)
    acc_new = alpha * acc_prev + _dot(p.astype(BF16), v)
    return m_new, l_new, acc_new


def _diff_finish(acc, l, lam, g, out_scale, rows):
    o = acc[:rows] / l[:rows] - lam * (acc[rows:] / l[rows:])
    return _rms(o, g) * out_scale


def _bf16_split3(x):
    hi = x.astype(BF16).astype(F32)
    mid = (x - hi).astype(BF16).astype(F32)
    lo = (x - hi - mid).astype(BF16).astype(F32)
    return hi, mid, lo


def _attn_kernel(qi_ref, kj_ref, qn_ref, kn_ref, q0_ref, k0_ref, qx_ref, kx_ref, v_ref, lam_ref, g_ref,
                 o_ref, s_sc, kf_sc, qf_sc, m_sc, l_sc, acc_sc, *, tq, tk, rc, hp, d_qk, n_heads, lam_init):
    t = pl.program_id(2)
    i, j = qi_ref[t], kj_ref[t]
    i_n, j_n = qn_ref[t], kn_ref[t]
    last_j = ((i + 1) * tq - 1) // tk
    dh = 2 * d_qk
    lane = lax.broadcasted_iota(jnp.int32, (1, dh), 1)
    lane_pos = lax.broadcasted_iota(jnp.int32, (1, tk), 1)

    def scores(q_ref, kx, hh, i_s, j_s, mp, r0, masked):
        qm = jnp.concatenate([q_ref[0, r0:r0 + rc, hh * dh:(hh + 1) * dh], qf_sc[hh, r0:r0 + rc, :]], axis=1)
        s = _dot_nt(qm, kx[mp])
        if masked:
            kpos = j_s * tk + lane_pos
            qpos = i_s * tq + r0 + lax.broadcasted_iota(jnp.int32, (rc, 1), 0)
            s = jnp.where(kpos <= qpos, s, NEG_INF)
        return s

    def k_operands(k_ref, i_s, j_s):
        off = jnp.asarray(j_s * tk - i_s * tq, F32).astype(BF16)
        kf = jnp.where((lane >= 6) & (lane < N_FEAT), off, kf_sc[...])
        out = []
        for hh in range(hp):
            k1, k2 = _split_maps(k_ref[0, :, hh * dh:(hh + 1) * dh], d_qk)
            out.append([jnp.concatenate([km, kf], axis=1) for km in (k1, k2)])
        return out

    heads = [pl.program_id(1) * hp + hh for hh in range(hp)]
    chunks = [(hh, mp, r0) for hh in range(hp) for mp in range(2) for r0 in range(0, tq, rc)]
    row0 = lambda hh, mp, r0: (hh * 2 + mp) * tq + r0

    @pl.when(t == 0)
    def _():
        jrel = lax.broadcasted_iota(jnp.int32, (tk, dh), 0)
        jlo = (jrel & 255).astype(F32)
        jhi = (jrel - (jrel & 255)).astype(F32)
        kf_sc[...] = jnp.where(lane < 3, jlo, jnp.where(lane < 6, jhi, 0.0)).astype(BF16)
        for hh in range(hp):
            s3 = _bf16_split3(jnp.full((1, dh), _alibi_slope(heads[hh], n_heads) * LOG2E, F32))
            row = jnp.where(lane % 3 == 0, s3[0], jnp.where(lane % 3 == 1, s3[1], s3[2]))
            row = jnp.where(lane < N_FEAT, row, 0.0)
            qf_sc[hh] = jnp.broadcast_to(row, (tq, dh)).astype(BF16)
        kx = k_operands(k0_ref, 0, 0)
        for hh, mp, r0 in chunks:
            s_sc[pl.ds(row0(hh, mp, r0), rc), :] = scores(q0_ref, kx[hh], hh, 0, 0, mp, r0, True)

    @pl.when(j == 0)
    def _():
        m_sc[...] = jnp.full(m_sc.shape, NEG_INF, F32)
        l_sc[...] = jnp.zeros(l_sc.shape, F32)
        acc_sc[...] = jnp.zeros(acc_sc.shape, F32)

    def step(next_masked):
        kx = k_operands(kx_ref, i_n, j_n)
        v = [v_ref[0, :, hh * dh:(hh + 1) * dh] for hh in range(hp)]
        for hh, mp, r0 in chunks:
            rows = pl.ds(row0(hh, mp, r0), rc)
            s = s_sc[rows, :]
            s_sc[rows, :] = scores(qx_ref, kx[hh], hh, i_n, j_n, mp, r0, next_masked)
            tiles = [s[:, c * LANES:(c + 1) * LANES] for c in range(tk // LANES)]
            m_cur = jnp.max(functools.reduce(jnp.maximum, tiles), axis=-1, keepdims=True)
            m_prev = m_sc[rows, :]
            m_new = jnp.maximum(m_prev, m_cur)
            alpha = jnp.exp2(m_prev - m_new)
            p_tiles = [jnp.exp2(tl - m_new) for tl in tiles]
            l_sc[rows, :] = alpha * l_sc[rows, :] + functools.reduce(jnp.add, p_tiles)
            p = jnp.concatenate([pt.astype(BF16) for pt in p_tiles], axis=1)
            acc_sc[rows, :] = alpha * acc_sc[rows, :] + _dot(p, v[hh])
            m_sc[rows, :] = m_new

    next_masked = (j_n + 1) * tk - 1 > i_n * tq

    @pl.when(next_masked)
    def _():
        step(True)

    @pl.when(jnp.logical_not(next_masked))
    def _():
        step(False)

    @pl.when(j == last_j)
    def _():
        lam = _lambda(lam_ref, lam_init)
        outs = []
        for hh in range(hp):
            rows = pl.ds(hh * 2 * tq, 2 * tq)
            l = jnp.sum(l_sc[rows, :], axis=-1, keepdims=True)
            outs.append(_diff_finish(acc_sc[rows, :], l, lam, g_ref[...], 1.0 - lam_init, tq))
        o_ref[0] = jnp.concatenate(outs, axis=1).astype(o_ref.dtype)


def _attn_prompt(qa, ka, va, lam_params, subln_g, *, n_heads, d_qk, lam_init):
    b, l, w = qa.shape
    dh = w // n_heads
    hp = 2 if n_heads % 2 == 0 else 1
    tq = _pick_tile(l, ATTN_Q_ROWS)
    tk = _pick_tile(l, ATTN_KV_ROWS)
    pairs = [(i, j) for i in range(l // tq) for j in range(((i + 1) * tq - 1) // tk + 1)]
    offs = np.asarray([j * tk - i * tq for i, j in pairs], np.float32)
    assert np.array_equal(offs.astype(jnp.bfloat16).astype(np.float32), offs) and d_qk >= N_FEAT
    nxt = pairs[1:] + pairs[-1:]
    tables = [jnp.asarray(col, jnp.int32) for col in
              ([p[0] for p in pairs], [p[1] for p in pairs], [p[0] for p in nxt], [p[1] for p in nxt])]

    def blk(rows, index):
        return pl.BlockSpec((1, rows, hp * dh), index)

    grid_spec = pltpu.PrefetchScalarGridSpec(
        num_scalar_prefetch=4,
        grid=(b, n_heads // hp, len(pairs)),
        in_specs=[
            blk(tq, lambda bb, hg, t, qi, kj, qn, kn: (bb, 0, hg)),
            blk(tk, lambda bb, hg, t, qi, kj, qn, kn: (bb, 0, hg)),
            blk(tq, lambda bb, hg, t, qi, kj, qn, kn: (bb, qn[t], hg)),
            blk(tk, lambda bb, hg, t, qi, kj, qn, kn: (bb, kn[t], hg)),
            blk(tk, lambda bb, hg, t, qi, kj, qn, kn: (bb, kj[t], hg)),
            pl.BlockSpec(lam_params.shape, lambda bb, hg, t, qi, kj, qn, kn: (0, 0)),
            pl.BlockSpec(subln_g.shape, lambda bb, hg, t, qi, kj, qn, kn: (0, 0)),
        ],
        out_specs=blk(tq, lambda bb, hg, t, qi, kj, qn, kn: (bb, qi[t], hg)),
        scratch_shapes=[pltpu.VMEM((hp * 2 * tq, tk), F32), pltpu.VMEM((tk, dh), BF16),
                        pltpu.VMEM((hp, tq, dh), BF16), pltpu.VMEM((hp * 2 * tq, LANES), F32), pltpu.VMEM((hp * 2 * tq, LANES), F32),
                        pltpu.VMEM((hp * 2 * tq, dh), F32)],
    )
    rc = _pick_tile(tq, ATTN_ROW_CHUNK)
    return pl.pallas_call(
        functools.partial(_attn_kernel, tq=tq, tk=tk, rc=rc, hp=hp, d_qk=d_qk, n_heads=n_heads,
                          lam_init=lam_init),
        grid_spec=grid_spec,
        out_shape=jax.ShapeDtypeStruct((b, l, w), BF16),
        compiler_params=_cparams(("parallel", "parallel", "arbitrary")),
        name="attn_prompt",
    )(*tables, qa, ka, qa, ka, va, lam_params, subln_g)


def _attn_decode_kernel(pt_ref, q_ref, kn_ref, vn_ref, lam_ref, g_ref, *rest,
                        pp, n_heads, d_qk, lam_init, past_len):
    k_refs = rest[:pp]
    v_refs = rest[pp:2 * pp]
    o_ref, m_sc, l_sc, acc_sc = rest[2 * pp:]
    c = pl.program_id(1)
    nq = q_ref.shape[1]
    dh = q_ref.shape[2] // n_heads
    page = k_refs[0].shape[0] // n_heads

    def head_rows(ref, h):
        return ref[pl.ds(h, page, stride=n_heads), :].astype(BF16)

    @pl.when(c == 0)
    def _():
        m_sc[...] = jnp.full(m_sc.shape, NEG_INF, F32)
        l_sc[...] = jnp.zeros(l_sc.shape, F32)
        acc_sc[...] = jnp.zeros(acc_sc.shape, F32)

    def q_rows(h):
        qh = q_ref[0][:, h * dh:(h + 1) * dh].astype(F32)
        q1, q2 = _split_maps(qh, d_qk)
        return jnp.concatenate([q1, q2], axis=0).astype(BF16)

    heads = range(n_heads)
    slopes = [jnp.float32(2.0 ** (-8.0 * (h + 1) / n_heads) * LOG2E) for h in heads]
    qh = [q_rows(h) for h in heads]
    rel = (c * (pp * page) - past_len + lax.broadcasted_iota(jnp.int32, (1, pp * page), 1)).astype(F32)
    s = [jnp.concatenate([_dot_nt(qh[h], head_rows(k_refs[p], h)) for p in range(pp)], axis=1) + slopes[h] * rel
         for h in heads]
    m_prev = [m_sc[h] for h in heads]
    m_new = [jnp.maximum(m_prev[h], jnp.max(s[h], axis=-1, keepdims=True)) for h in heads]
    alpha = [jnp.exp2(m_prev[h] - m_new[h]) for h in heads]
    p_all = [jnp.exp2(s[h] - m_new[h]) for h in heads]
    p16 = [p.astype(BF16) for p in p_all]
    pv = [functools.reduce(jnp.add, [_dot(p16[h][:, p * page:(p + 1) * page], head_rows(v_refs[p], h))
                                     for p in range(pp)]) for h in heads]
    for h in heads:
        l_sc[h] = alpha[h] * l_sc[h] + jnp.sum(p_all[h], axis=-1, keepdims=True)
        acc_sc[h] = alpha[h] * acc_sc[h] + pv[h]
        m_sc[h] = m_new[h]

    @pl.when(c == pl.num_programs(1) - 1)
    def _():
        lam = _lambda(lam_ref, lam_init)
        outs = []
        for h in range(n_heads):
            slope = jnp.float32(2.0 ** (-8.0 * (h + 1) / n_heads) * LOG2E)
            pad = jnp.zeros((page - nq, dh), F32)
            kn = jnp.concatenate([kn_ref[:, h, :], pad], axis=0).astype(BF16)
            vn = jnp.concatenate([vn_ref[:, h, :], pad], axis=0).astype(BF16)
            s = _dot_nt(q_rows(h), kn)
            jj = lax.broadcasted_iota(jnp.int32, (1, page), 1)
            r = lax.broadcasted_iota(jnp.int32, (2 * nq, 1), 0)
            ii = jnp.where(r >= nq, r - nq, r)
            s = jnp.where(jj <= ii, s + slope * jj.astype(F32), NEG_INF)
            _, l_new, acc_new = _softmax_update(s, vn, m_sc[h], l_sc[h], acc_sc[h])
            outs.append(_diff_finish(acc_new, l_new, lam, g_ref[...], 1.0 - lam_init, nq))
        o_ref[0] = jnp.concatenate(outs, axis=1).astype(o_ref.dtype)


def _attn_decode(qa, k_new, v_new, cache_k, cache_v, layer, page_table, lam_params, subln_g,
                 *, n_heads, d_qk, lam_init):
    b, nq, w = qa.shape
    n_pages = page_table.shape[1]
    pp = math.gcd(n_pages, DECODE_PAGES)
    dh = cache_k.shape[3]
    page = cache_k.shape[2] // n_heads

    def page_spec(p):
        return pl.BlockSpec((None, None, page * n_heads, dh),
                            lambda bb, c, pt: (layer, pt[bb * n_pages + c * pp + p], 0, 0))

    def tok_spec():
        return pl.BlockSpec((1, nq, w), lambda bb, c, pt: (bb, 0, 0))

    def new_spec():
        return pl.BlockSpec((None, nq, n_heads, dh), lambda bb, c, pt: (bb, 0, 0, 0))

    grid_spec = pltpu.PrefetchScalarGridSpec(
        num_scalar_prefetch=1,
        grid=(b, n_pages // pp),
        in_specs=[tok_spec(), new_spec(), new_spec(),
                  pl.BlockSpec(lam_params.shape, lambda bb, c, pt: (0, 0)),
                  pl.BlockSpec(subln_g.shape, lambda bb, c, pt: (0, 0))]
                 + [page_spec(p) for p in range(pp)] + [page_spec(p) for p in range(pp)],
        out_specs=tok_spec(),
        scratch_shapes=[pltpu.VMEM((n_heads, 2 * nq, 1), F32), pltpu.VMEM((n_heads, 2 * nq, 1), F32),
                        pltpu.VMEM((n_heads, 2 * nq, w // n_heads), F32)],
    )
    return pl.pallas_call(
        functools.partial(_attn_decode_kernel, pp=pp, n_heads=n_heads, d_qk=d_qk, lam_init=lam_init,
                          past_len=n_pages * page),
        grid_spec=grid_spec,
        out_shape=jax.ShapeDtypeStruct((b, nq, w), BF16),
        compiler_params=_cparams(("parallel", "arbitrary")),
        name="attn_decode",
    )(page_table.reshape(-1), qa, k_new, v_new, lam_params, subln_g,
      *([cache_k] * pp), *([cache_v] * pp))


def _cumsum_rows(x):
    c = x.shape[0]
    tri = (lax.broadcasted_iota(jnp.int32, (c, c), 0) >= lax.broadcasted_iota(jnp.int32, (c, c), 1))
    tri = jnp.where(tri, 1.0, 0.0).astype(BF16)
    hi = x.astype(BF16)
    r1 = x - hi.astype(F32)
    mid = r1.astype(BF16)
    lo = (r1 - mid.astype(F32)).astype(BF16)
    return _dot(tri, hi) + _dot(tri, mid) + _dot(tri, lo)


def _hgrn_kernel(q_ref, k_ref, v_ref, lf_ref, sg_ref, g_ref, o_ref, s_ref, st_sc, *, n_heads):
    c_idx = pl.program_id(1)
    c = q_ref.shape[1]
    dk = q_ref.shape[2] // n_heads
    dv = v_ref.shape[2] // n_heads
    t_sub = min(SUB_CHUNK, c)
    n_sub = c // t_sub

    @pl.when(c_idx == 0)
    def _():
        st_sc[...] = jnp.zeros(st_sc.shape, F32)

    q = q_ref[0].astype(F32)
    k = k_ref[0].astype(F32)
    v16 = v_ref[0]
    b = _cumsum_rows(lf_ref[0])
    anchors = [b[i * t_sub + t_sub // 2 - 1:i * t_sub + t_sub // 2, :] for i in range(n_sub)]
    anchor_rows = jnp.concatenate([jnp.broadcast_to(a, (t_sub, a.shape[1])) for a in anchors], axis=0)
    q_intra = (q * jnp.exp(jnp.minimum(b - anchor_rows, EXP_CLAMP))).astype(BF16)
    q_inter = (q * jnp.exp(b)).astype(BF16)
    b_last = b[c - 1:c, :]
    k_state = (k * jnp.exp(b_last - b)).astype(BF16)
    decay = jnp.exp(b_last)
    k_parts = []
    for i, a in enumerate(anchors):
        hi = (i + 1) * t_sub
        k_parts.append((k[:hi] * jnp.exp(jnp.minimum(a - b[:hi], EXP_CLAMP))).astype(BF16))
        if hi < c:
            k_parts.append(jnp.zeros((c - hi, k.shape[1]), BF16))
    k_intra = jnp.concatenate(k_parts, axis=0)

    heads = range(n_heads)
    ks = [slice(h * dk, (h + 1) * dk) for h in heads]
    vs = [slice(h * dv, (h + 1) * dv) for h in heads]
    tt = lax.broadcasted_iota(jnp.int32, (c, c), 0)
    ss = lax.broadcasted_iota(jnp.int32, (c, c), 1)
    wide = [_dot_nt(q_intra[:, ks[h]], k_intra[:, ks[h]]) for h in heads]
    a_mat = [jnp.concatenate([w[i * t_sub:(i + 1) * t_sub, i * c:(i + 1) * c] for i in range(n_sub)], axis=0)
             for w in wide]
    a_mat = [jnp.where(ss <= tt, a, 0.0).astype(BF16) for a in a_mat]
    st = [st_sc[h] for h in heads]
    o = [_dot_nt(q_inter[:, ks[h]], st[h].astype(BF16)) + _dot(a_mat[h], v16[:, vs[h]]) for h in heads]
    for h in heads:
        st_sc[h] = decay[:, ks[h]] * st[h] + _dot_tn(v16[:, vs[h]], k_state[:, ks[h]])
    outs = [_rms(o[h], g_ref[...]) * sg_ref[0][:, vs[h]].astype(F32) for h in heads]
    o_ref[0] = jnp.concatenate(outs, axis=1).astype(o_ref.dtype)

    @pl.when(c_idx == pl.num_programs(1) - 1)
    def _():
        for h in range(n_heads):
            s_ref[0, h] = st_sc[h].T


def _hgrn_prompt(qb, kb, vb, lf, sg, norm_g, *, n_heads):
    b, l, wk = qb.shape
    wv = vb.shape[2]
    c = _pick_tile(l, HGRN_CHUNK)
    dk, dv = wk // n_heads, wv // n_heads

    def tok(width):
        return pl.BlockSpec((1, c, width), lambda bb, cc: (bb, cc, 0))

    return pl.pallas_call(
        functools.partial(_hgrn_kernel, n_heads=n_heads),
        grid=(b, l // c),
        in_specs=[tok(wk), tok(wk), tok(wv), tok(wk), tok(wv), pl.BlockSpec(norm_g.shape, lambda bb, cc: (0, 0))],
        out_specs=[tok(wv), pl.BlockSpec((1, n_heads, dk, dv), lambda bb, cc: (bb, 0, 0, 0))],
        out_shape=[jax.ShapeDtypeStruct((b, l, wv), BF16), jax.ShapeDtypeStruct((b, n_heads, dk, dv), F32)],
        scratch_shapes=[pltpu.VMEM((n_heads, dv, dk), F32)],
        compiler_params=_cparams(("parallel", "arbitrary")),
        name="hgrn_prompt",
    )(qb, kb, vb, lf, sg, norm_g)


def _hgrn_decode_kernel(q_ref, k_ref, v_ref, lf_ref, sg_ref, g_ref, s0_ref, o_ref, s_ref, *, n_heads):
    n = q_ref.shape[1]
    dk = q_ref.shape[2] // n_heads
    dv = v_ref.shape[2] // n_heads
    row = lax.broadcasted_iota(jnp.int32, (n, 1), 0)
    pad = 2 * 8 - n if n < 2 * 8 else 0
    outs = []
    for h in range(n_heads):
        ks = slice(h * dk, (h + 1) * dk)
        vs = slice(h * dv, (h + 1) * dv)
        q = q_ref[0][:, ks].astype(F32)
        k = k_ref[0][:, ks].astype(F32)
        v = v_ref[0][:, vs].astype(F32)
        lf = lf_ref[0][:, ks]
        st = s0_ref[0, h].T
        b = lf
        for sh in [1 << e for e in range((n - 1).bit_length())]:
            shifted = jnp.concatenate([jnp.zeros((sh, dk), F32), b[:n - sh]], axis=0)
            b = b + shifted

        def padded(x):
            return jnp.concatenate([x, jnp.zeros((pad, x.shape[1]), F32)], axis=0) if pad else x

        o = _dot_nt(padded(q * jnp.exp(b)).astype(BF16), st.astype(BF16))[:n]
        for s in range(n):
            w = jnp.sum(q * k[s:s + 1] * jnp.exp(jnp.minimum(b - b[s:s + 1], 0.0)), axis=-1, keepdims=True)
            o = o + jnp.where(row >= s, w, 0.0) * v[s:s + 1]
        b_last = b[n - 1:n]
        kd = padded(k * jnp.exp(b_last - b)).astype(BF16)
        st_new = jnp.exp(b_last) * st + _dot_tn(padded(v).astype(BF16), kd)
        s_ref[0, h] = st_new.T
        outs.append(_rms(o, g_ref[...]) * sg_ref[0][:, vs].astype(F32))
    o_ref[0] = jnp.concatenate(outs, axis=1).astype(o_ref.dtype)


def _hgrn_decode(qb, kb, vb, lf, sg, norm_g, state, layer, *, n_heads):
    b, n, wk = qb.shape
    wv = vb.shape[2]
    dk, dv = wk // n_heads, wv // n_heads

    def tok(width):
        return pl.BlockSpec((1, n, width), lambda bb: (bb, 0, 0))

    st = pl.BlockSpec((1, n_heads, dk, dv), lambda bb: (bb, 0, 0, 0))
    st_in = pl.BlockSpec((None, 1, n_heads, dk, dv), lambda bb: (layer, bb, 0, 0, 0))
    return pl.pallas_call(
        functools.partial(_hgrn_decode_kernel, n_heads=n_heads),
        grid=(b,),
        in_specs=[tok(wk), tok(wk), tok(wv), tok(wk), tok(wv), pl.BlockSpec(norm_g.shape, lambda bb: (0, 0)), st_in],
        out_specs=[tok(wv), st],
        out_shape=[jax.ShapeDtypeStruct((b, n, wv), BF16), jax.ShapeDtypeStruct((b, n_heads, dk, dv), F32)],
        compiler_params=_cparams(("parallel",)),
        name="hgrn_decode",
    )(qb, kb, vb, lf, sg, norm_g, state)


def _merge_kernel(h_ref, oa_ref, ob_ref, ga_ref, gb_ref, wa_ref, wb_ref, wo_ref, g_ref, o_ref):
    m = (ga_ref[...].astype(F32) * _dot(oa_ref[...], wa_ref[...])
         + gb_ref[...].astype(F32) * _dot(ob_ref[...], wb_ref[...]))
    o_ref[...] = h_ref[...] + _rms(_dot(m.astype(BF16), wo_ref[...]), g_ref[...])


def _merge(h, oa, ob, ga, gb, w_proj_a, w_proj_b, w_out, post_g):
    n, d = h.shape
    tm = _pick_tile(n, MERGE_ROWS)

    def row(width):
        return pl.BlockSpec((tm, width), lambda i: (i, 0))

    return pl.pallas_call(
        _merge_kernel,
        grid=(n // tm,),
        in_specs=[row(d), row(oa.shape[1]), row(ob.shape[1]), row(d), row(d),
                  _resident(w_proj_a.shape), _resident(w_proj_b.shape), _resident(w_out.shape), _resident((1, d))],
        out_specs=row(d),
        out_shape=jax.ShapeDtypeStruct((n, d), F32),
        compiler_params=_cparams(("parallel",)),
        name="merge",
    )(h, oa, ob, ga, gb, w_proj_a, w_proj_b, w_out, post_g)


def kernel(x_prompt, x_sample, cache_k, cache_v, state_s, page_table, ffn1_pre_g, ffn1_post_g, ffn1_w_gu, ffn1_w_down, mix_pre_g, mix_post_g, w_in, lambda_q1, lambda_k1, lambda_q2, lambda_k2, attn_subln_g, hgrn_lb_logits, hgrn_norm_g, w_proj_a, w_proj_b, w_out, ffn2_pre_g, ffn2_post_g, ffn2_w_gu, ffn2_w_down):
    depth, d_model = ffn1_pre_g.shape
    bp, lp, _ = x_prompt.shape
    bs, ls, _ = x_sample.shape
    n_heads_a, two_dqk = cache_k.shape[3], cache_k.shape[4]
    d_qk = two_dqk // 2
    d_v_a = cache_v.shape[4]
    n_heads_b, d_k_b, d_v_b = state_s.shape[2:]
    w_qk, w_a = n_heads_a * two_dqk, n_heads_a * d_v_a
    w_kb, w_b = n_heads_b * d_k_b, n_heads_b * d_v_b
    widths = (w_qk, w_qk, w_a, w_kb, w_kb, w_b, w_b, d_model, d_model)
    assert sum(widths) == w_in.shape[2] and two_dqk == d_v_a == LANES and cache_k.shape[2] == PAGE_SIZE

    cache_k2 = cache_k.reshape(depth, cache_k.shape[1], PAGE_SIZE * n_heads_a, two_dqk)
    cache_v2 = cache_v.reshape(depth, cache_v.shape[1], PAGE_SIZE * n_heads_a, d_v_a)
    yp = x_prompt.reshape(bp * lp, d_model)
    ys = x_sample.reshape(bs * ls, d_model)
    kp_l, vp_l, sp_l, ks_l, vs_l, ss_l = [], [], [], [], [], []
    for l in range(depth):
        lam_init = 0.8 - 0.6 * math.exp(-0.3 * l)
        row = lambda g: g[l].reshape(1, -1)
        wgu1, wdn1 = ffn1_w_gu[l].astype(BF16), ffn1_w_down[l].astype(BF16)
        wgu2, wdn2 = ffn2_w_gu[l].astype(BF16), ffn2_w_down[l].astype(BF16)
        win = w_in[l].astype(BF16)
        wpa, wpb, wo = w_proj_a[l].astype(BF16), w_proj_b[l].astype(BF16), w_out[l].astype(BF16)
        lam_params = jnp.stack([lambda_q1[l], lambda_k1[l], lambda_q2[l], lambda_k2[l]])
        subln_g, norm_g = row(attn_subln_g), row(hgrn_norm_g)

        def mixer_inputs(y):
            h = _ffn(y, row(ffn1_pre_g), row(ffn1_post_g), wgu1, wdn1)
            return h, _proj(h, row(mix_pre_g), win, hgrn_lb_logits, widths, l, d_qk ** -0.5 * LOG2E, n_heads_a)

        def mixer_outputs(h, oa, ob, ga, gb):
            h2 = _merge(h, oa, ob, ga, gb, wpa, wpb, wo, row(mix_post_g))
            return _ffn(h2, row(ffn2_pre_g), row(ffn2_post_g), wgu2, wdn2)

        h, (qa, ka, ka16, va, va16, qb, lf, kb, vb, sg, ga, gb) = mixer_inputs(yp)
        seq = lambda a: a.reshape(bp, lp, a.shape[1])
        oa = _attn_prompt(seq(qa), seq(ka16), seq(va16), lam_params, subln_g,
                          n_heads=n_heads_a, d_qk=d_qk, lam_init=lam_init)
        ob, sp = _hgrn_prompt(seq(qb), seq(kb), seq(vb), seq(lf), seq(sg), norm_g, n_heads=n_heads_b)
        yp = mixer_outputs(h, oa.reshape(bp * lp, w_a), ob.reshape(bp * lp, w_b), ga, gb)
        kp_l.append(ka.reshape(bp, lp, n_heads_a, two_dqk))
        vp_l.append(va.reshape(bp, lp, n_heads_a, d_v_a))
        sp_l.append(sp)

        h, (qa, ka, ka16, va, va16, qb, lf, kb, vb, sg, ga, gb) = mixer_inputs(ys)
        seq = lambda a: a.reshape(bs, ls, a.shape[1])
        ka_s = ka.reshape(bs, ls, n_heads_a, two_dqk)
        va_s = va.reshape(bs, ls, n_heads_a, d_v_a)
        oa = _attn_decode(seq(qa), ka_s, va_s, cache_k2, cache_v2, l, page_table, lam_params, subln_g,
                          n_heads=n_heads_a, d_qk=d_qk, lam_init=lam_init)
        ob, ss = _hgrn_decode(seq(qb), seq(kb), seq(vb), seq(lf), seq(sg), norm_g, state_s, l, n_heads=n_heads_b)
        ys = mixer_outputs(h, oa.reshape(bs * ls, w_a), ob.reshape(bs * ls, w_b), ga, gb)
        ks_l.append(ka_s)
        vs_l.append(va_s)
        ss_l.append(ss)

    def layers(xs):
        return xs[0][None] if len(xs) == 1 else jnp.stack(xs)

    return (yp.reshape(bp, lp, d_model), ys.reshape(bs, ls, d_model),
            layers(kp_l), layers(vp_l), layers(sp_l), layers(ks_l), layers(vs_l), layers(ss_l))
```

```python
import functools
import math

import numpy as np
import jax
import jax.numpy as jnp
from jax import lax
from jax.experimental import pallas as pl
from jax.experimental.pallas import tpu as pltpu

F32 = jnp.float32
BF16 = jnp.bfloat16
EPS = 1e-6
NEG_INF = -1e30
LOG2E = math.log2(math.e)
PAGE_SIZE = 128
LANES = 128
SUB_CHUNK = 16
EXP_CLAMP = 80.0
N_FEAT = 9
VMEM_LIMIT = 56 * 1024 * 1024
FFN_ROWS = 1024
PROJ_ROWS = 512
MERGE_ROWS = 1024
ATTN_Q_ROWS = 512
ATTN_KV_ROWS = 512
ATTN_HEADS_PER_STEP = 4
HGRN_CHUNK = 128
HGRN_STEP_CHUNKS = 4
DECODE_PAGES = 32


def _cparams(sem):
    return pltpu.CompilerParams(dimension_semantics=sem, vmem_limit_bytes=VMEM_LIMIT)


def _resident(shape):
    nd = len(shape)
    return pl.BlockSpec(shape, lambda *_: (0,) * nd, pipeline_mode=pl.Buffered(1))


def _rms(x, g):
    return x * lax.rsqrt(jnp.mean(x * x, axis=-1, keepdims=True) + EPS) * g


def _dot(a, b):
    return jnp.dot(a, b, preferred_element_type=F32)


def _dot_nt(a, b):
    return lax.dot_general(a, b, (((1,), (1,)), ((), ())), preferred_element_type=F32)


def _dot_tn(a, b):
    return lax.dot_general(a, b, (((0,), (0,)), ((), ())), preferred_element_type=F32)


def _pick_tile(n, target):
    t = math.gcd(n, target)
    assert t % 8 == 0 or t == n, (n, target)
    return t


def _ffn_kernel(x_ref, pre_ref, post_ref, wgu_ref, wdn_ref, o_ref, *, d_ff, ck):
    x = x_ref[...]
    xn = _rms(x, pre_ref[...]).astype(BF16)
    acc = None
    for c in range(d_ff // ck):
        g = _dot(xn, wgu_ref[:, c * ck:(c + 1) * ck])
        u = _dot(xn, wgu_ref[:, d_ff + c * ck:d_ff + (c + 1) * ck])
        a = (g * jax.nn.sigmoid(g) * u).astype(BF16)
        part = _dot(a, wdn_ref[c * ck:(c + 1) * ck, :])
        acc = part if acc is None else acc + part
    o_ref[...] = x + 0.5 * _rms(acc, post_ref[...])


def _ffn(x, pre_g, post_g, w_gu, w_down):
    n, d = x.shape
    d_ff = w_down.shape[0]
    tm = _pick_tile(n, FFN_ROWS)
    ck = 2 * LANES if d_ff % (2 * LANES) == 0 else d_ff
    row = pl.BlockSpec((tm, d), lambda i: (i, 0))
    return pl.pallas_call(
        functools.partial(_ffn_kernel, d_ff=d_ff, ck=ck),
        grid=(n // tm,),
        in_specs=[row, _resident((1, d)), _resident((1, d)), _resident(w_gu.shape), _resident(w_down.shape)],
        out_specs=row,
        out_shape=jax.ShapeDtypeStruct((n, d), F32),
        compiler_params=_cparams(("parallel",)),
        name="ffn",
    )(x, pre_g, post_g, w_gu, w_down)


def _proj_kernel(h_ref, g_ref, win_ref, lbl_ref,
                 qa_ref, ka_ref, ka16_ref, va_ref, vat_ref,
                 qb_ref, lf_ref, kb_ref, vb_ref, sg_ref, ga_ref, gb_ref, *, offs, layer, q_scale):
    u = _rms(h_ref[...], g_ref[...]).astype(BF16)

    def z(i):
        return _dot(u, win_ref[:, offs[i]:offs[i + 1]])

    def store_heads(ref, x):
        dh = ref.shape[1]
        nh = x.shape[1] // dh
        for hh in range(nh):
            ref[pl.ds(hh, x.shape[0], stride=nh), :] = x[:, hh * dh:(hh + 1) * dh]

    qa_ref[...] = (z(0) * q_scale).astype(BF16)
    k = z(1)
    store_heads(ka_ref, k)
    ka16_ref[...] = k.astype(BF16)
    v = z(2)
    store_heads(va_ref, v)
    vat_ref[0] = v.T.astype(BF16)
    qb = z(3)
    qb_ref[...] = (qb * jax.nn.sigmoid(qb)).astype(BF16)
    lbl = lbl_ref[...]
    e = jnp.exp(lbl - jnp.max(lbl, axis=0, keepdims=True))
    lb = jnp.sum(e[:layer + 1], axis=0, keepdims=True) / jnp.sum(e, axis=0, keepdims=True)
    fr = z(4)
    lf_ref[...] = jnp.log(lb + (1.0 - lb) * jax.nn.sigmoid(fr))
    kb_ref[...] = ((1.0 - lb) * jax.nn.sigmoid(-fr)).astype(BF16)
    vb_ref[...] = z(5).astype(BF16)
    gbr = z(6)
    sg_ref[...] = (gbr * jax.nn.sigmoid(gbr)).astype(BF16)
    ga_ref[...] = z(7).astype(BF16)
    gb_ref[...] = z(8).astype(BF16)


def _proj(h, mix_pre_g, w_in, lb_logits, widths, layer, q_scale, n_heads_a, seq_len):
    n, d = h.shape
    tm = _pick_tile(n, PROJ_ROWS)
    offs = tuple(int(o) for o in np.concatenate([[0], np.cumsum(widths)]))
    w = widths
    if seq_len % tm:
        seq_len = n
    nb = seq_len // tm

    def out(width, dtype):
        return pl.BlockSpec((tm, width), lambda i: (i, 0)), jax.ShapeDtypeStruct((n, width), dtype)

    def out_heads(width):
        dh = width // n_heads_a
        return (pl.BlockSpec((tm * n_heads_a, dh), lambda i: (i, 0)),
                jax.ShapeDtypeStruct((n * n_heads_a, dh), F32))

    def out_transposed(width):
        return (pl.BlockSpec((1, width, tm), lambda i: (i // nb, 0, i % nb)),
                jax.ShapeDtypeStruct((n // seq_len, width, seq_len), BF16))

    outs = [out(w[0], BF16), out_heads(w[1]), out(w[1], BF16), out_heads(w[2]), out_transposed(w[2]),
            out(w[3], BF16), out(w[4], F32), out(w[4], BF16), out(w[5], BF16), out(w[6], BF16),
            out(w[7], BF16), out(w[8], BF16)]
    return pl.pallas_call(
        functools.partial(_proj_kernel, offs=offs, layer=layer, q_scale=q_scale),
        grid=(n // tm,),
        in_specs=[pl.BlockSpec((tm, d), lambda i: (i, 0)), _resident((1, d)), _resident(w_in.shape),
                  _resident(lb_logits.shape)],
        out_specs=[o[0] for o in outs],
        out_shape=[o[1] for o in outs],
        compiler_params=_cparams(("parallel",)),
        name="proj",
    )(h, mix_pre_g, w_in, lb_logits)


def _alibi_slope(h, n_heads):
    slope = jnp.float32(0.0)
    for hh in range(n_heads):
        slope = jnp.where(h == hh, jnp.float32(2.0 ** (-8.0 * (hh + 1) / n_heads)), slope)
    return slope


def _lambda(lam_ref, lam_init):
    lp = lam_ref[...]
    l1 = jnp.exp(jnp.sum(lp[0:1] * lp[1:2], axis=-1, keepdims=True))
    l2 = jnp.exp(jnp.sum(lp[2:3] * lp[3:4], axis=-1, keepdims=True))
    return l1 - l2 + lam_init


def _split_maps(x, d_qk):
    lane = lax.broadcasted_iota(jnp.int32, x.shape, 1)
    zero = jnp.zeros_like(x)
    return jnp.where(lane < d_qk, x, zero), jnp.where(lane >= d_qk, x, zero)


def _softmax_update(s, v, m_prev, l_prev, acc_prev):
    m_new = jnp.maximum(m_prev, jnp.max(s, axis=-1, keepdims=True))
    alpha = jnp.exp2(m_prev - m_new)
    p = jnp.exp2(s - m_new)
    l_new = alpha * l_prev + jnp.sum(p, axis=-1, keepdims=True)
    acc_new = alpha * acc_prev + _dot(p.astype(BF16), v)
    return m_new, l_new, acc_new


def _diff_finish(acc, l, lam, g, out_scale, rows):
    o = acc[:rows] / l[:rows] - lam * (acc[rows:] / l[rows:])
    return _rms(o, g) * out_scale


def _bf16_split3(x):
    hi = x.astype(BF16).astype(F32)
    mid = (x - hi).astype(BF16).astype(F32)
    lo = (x - hi - mid).astype(BF16).astype(F32)
    return hi, mid, lo


def _attn_kernel(qi_ref, kj_ref, qn_ref, kn_ref, q0_ref, k0_ref, qx_ref, kx_ref, vt_ref, lam_ref, g_ref,
                 o_ref, s_sc, kf_sc, qf_sc, m_sc, l_sc, acc_sc, *, tq, tk, hp, d_qk, n_heads, lam_init):
    t = pl.program_id(2)
    i, j = qi_ref[t], kj_ref[t]
    i_n, j_n = qn_ref[t], kn_ref[t]
    last_j = ((i + 1) * tq - 1) // tk
    dh = 2 * d_qk
    lane = lax.broadcasted_iota(jnp.int32, (1, dh), 1)
    lane_pos = lax.broadcasted_iota(jnp.int32, (1, tq), 1)

    def scores(q_ref, kx, hh, i_s, j_s, mp, masked):
        qm = jnp.concatenate([q_ref[0, :, hh * dh:(hh + 1) * dh], qf_sc[hh]], axis=1)
        s = _dot_nt(kx[mp], qm)
        if masked:
            kpos = j_s * tk + lax.broadcasted_iota(jnp.int32, (tk, 1), 0)
            qpos = i_s * tq + lane_pos
            s = jnp.where(kpos <= qpos, s, NEG_INF)
        return s

    def k_operands(k_ref, i_s, j_s):
        off = jnp.asarray(j_s * tk - i_s * tq, F32).astype(BF16)
        kf = jnp.where((lane >= 6) & (lane < N_FEAT), off, kf_sc[...])
        out = []
        for hh in range(hp):
            k1, k2 = _split_maps(k_ref[0, :, hh * dh:(hh + 1) * dh], d_qk)
            out.append([jnp.concatenate([km, kf], axis=1) for km in (k1, k2)])
        return out

    heads = [pl.program_id(1) * hp + hh for hh in range(hp)]
    chains = [(hh, mp) for hh in range(hp) for mp in range(2)]

    @pl.when(t == 0)
    def _():
        jrel = lax.broadcasted_iota(jnp.int32, (tk, dh), 0)
        jlo = (jrel & 255).astype(F32)
        jhi = (jrel - (jrel & 255)).astype(F32)
        kf_sc[...] = jnp.where(lane < 3, jlo, jnp.where(lane < 6, jhi, 0.0)).astype(BF16)
        for hh in range(hp):
            s3 = _bf16_split3(jnp.full((1, dh), _alibi_slope(heads[hh], n_heads) * LOG2E, F32))
            row = jnp.where(lane % 3 == 0, s3[0], jnp.where(lane % 3 == 1, s3[1], s3[2]))
            row = jnp.where(lane < N_FEAT, row, 0.0)
            qf_sc[hh] = jnp.broadcast_to(row, (tq, dh)).astype(BF16)
        kx = k_operands(k0_ref, 0, 0)
        for c, (hh, mp) in enumerate(chains):
            s_sc[c] = scores(q0_ref, kx[hh], hh, 0, 0, mp, True)

    @pl.when(j == 0)
    def _():
        m_sc[...] = jnp.full(m_sc.shape, NEG_INF, F32)
        l_sc[...] = jnp.zeros(l_sc.shape, F32)
        acc_sc[...] = jnp.zeros(acc_sc.shape, F32)

    def step(next_masked):
        kx = k_operands(kx_ref, i_n, j_n)
        for c, (hh, mp) in enumerate(chains):
            s = s_sc[c]
            s_sc[c] = scores(qx_ref, kx[hh], hh, i_n, j_n, mp, next_masked)
            m_prev = m_sc[c]
            m_new = jnp.maximum(m_prev, jnp.max(s, axis=0, keepdims=True))
            alpha = jnp.exp2(m_prev - m_new)
            p = jnp.exp2(s - m_new)
            l_sc[c] = alpha * l_sc[c] + jnp.sum(p, axis=0, keepdims=True)
            acc_sc[c] = alpha * acc_sc[c] + _dot(vt_ref[0, hh * dh:(hh + 1) * dh, :], p.astype(BF16))
            m_sc[c] = m_new

    next_masked = (j_n + 1) * tk - 1 > i_n * tq

    @pl.when(next_masked)
    def _():
        step(True)

    @pl.when(jnp.logical_not(next_masked))
    def _():
        step(False)

    @pl.when(j == last_j)
    def _():
        lam = _lambda(lam_ref, lam_init)
        outs = []
        for hh in range(hp):
            c1, c2 = 2 * hh, 2 * hh + 1
            o_t = acc_sc[c1] / l_sc[c1] - lam * (acc_sc[c2] / l_sc[c2])
            outs.append(_rms(o_t.T, g_ref[...]) * (1.0 - lam_init))
        o_ref[0] = jnp.concatenate(outs, axis=1).astype(o_ref.dtype)


def _attn_prompt(qa, ka, va_t, lam_params, subln_g, *, n_heads, d_qk, lam_init):
    b, l, w = qa.shape
    dh = w // n_heads
    hp = math.gcd(n_heads, ATTN_HEADS_PER_STEP)
    tq = _pick_tile(l, ATTN_Q_ROWS)
    tk = _pick_tile(l, ATTN_KV_ROWS)
    pairs = [(i, j) for i in range(l // tq) for j in range(((i + 1) * tq - 1) // tk + 1)]
    offs = np.asarray([j * tk - i * tq for i, j in pairs], np.float32)
    assert np.array_equal(offs.astype(jnp.bfloat16).astype(np.float32), offs) and d_qk >= N_FEAT
    nxt = pairs[1:] + pairs[-1:]
    tables = [jnp.asarray(col, jnp.int32) for col in
              ([p[0] for p in pairs], [p[1] for p in pairs], [p[0] for p in nxt], [p[1] for p in nxt])]

    def blk(rows, index):
        return pl.BlockSpec((1, rows, hp * dh), index)

    grid_spec = pltpu.PrefetchScalarGridSpec(
        num_scalar_prefetch=4,
        grid=(b, n_heads // hp, len(pairs)),
        in_specs=[
            blk(tq, lambda bb, hg, t, qi, kj, qn, kn: (bb, 0, hg)),
            blk(tk, lambda bb, hg, t, qi, kj, qn, kn: (bb, 0, hg)),
            blk(tq, lambda bb, hg, t, qi, kj, qn, kn: (bb, qn[t], hg)),
            blk(tk, lambda bb, hg, t, qi, kj, qn, kn: (bb, kn[t], hg)),
            pl.BlockSpec((1, hp * dh, tk), lambda bb, hg, t, qi, kj, qn, kn: (bb, hg, kj[t])),
            pl.BlockSpec(lam_params.shape, lambda bb, hg, t, qi, kj, qn, kn: (0, 0)),
            pl.BlockSpec(subln_g.shape, lambda bb, hg, t, qi, kj, qn, kn: (0, 0)),
        ],
        out_specs=blk(tq, lambda bb, hg, t, qi, kj, qn, kn: (bb, qi[t], hg)),
        scratch_shapes=[pltpu.VMEM((hp * 2, tk, tq), F32), pltpu.VMEM((tk, dh), BF16),
                        pltpu.VMEM((hp, tq, dh), BF16), pltpu.VMEM((hp * 2, 1, tq), F32),
                        pltpu.VMEM((hp * 2, 1, tq), F32), pltpu.VMEM((hp * 2, dh, tq), F32)],
    )
    return pl.pallas_call(
        functools.partial(_attn_kernel, tq=tq, tk=tk, hp=hp, d_qk=d_qk, n_heads=n_heads,
                          lam_init=lam_init),
        grid_spec=grid_spec,
        out_shape=jax.ShapeDtypeStruct((b, l, w), BF16),
        compiler_params=_cparams(("parallel", "parallel", "arbitrary")),
        name="attn_prompt",
    )(*tables, qa, ka, qa, ka, va_t, lam_params, subln_g)


def _attn_decode_kernel(pt_ref, q_ref, kn_ref, vn_ref, lam_ref, g_ref, *rest,
                        pp, n_heads, d_qk, lam_init, past_len):
    k_refs = rest[:pp]
    v_refs = rest[pp:2 * pp]
    o_ref, m_sc, l_sc, acc_sc = rest[2 * pp:]
    c = pl.program_id(1)
    nq = q_ref.shape[1]
    dh = q_ref.shape[2] // n_heads
    page = k_refs[0].shape[0] // n_heads

    def head_rows(ref, h):
        return ref[pl.ds(h, page, stride=n_heads), :].astype(BF16)

    @pl.when(c == 0)
    def _():
        m_sc[...] = jnp.full(m_sc.shape, NEG_INF, F32)
        l_sc[...] = jnp.zeros(l_sc.shape, F32)
        acc_sc[...] = jnp.zeros(acc_sc.shape, F32)

    def q_rows(h):
        qh = q_ref[0][:, h * dh:(h + 1) * dh].astype(F32)
        q1, q2 = _split_maps(qh, d_qk)
        return jnp.concatenate([q1, q2], axis=0).astype(BF16)

    heads = range(n_heads)
    slopes = [jnp.float32(2.0 ** (-8.0 * (h + 1) / n_heads) * LOG2E) for h in heads]
    qh = [q_rows(h) for h in heads]
    rel = (c * (pp * page) - past_len + lax.broadcasted_iota(jnp.int32, (1, pp * page), 1)).astype(F32)
    s = [jnp.concatenate([_dot_nt(qh[h], head_rows(k_refs[p], h)) for p in range(pp)], axis=1) + slopes[h] * rel
         for h in heads]
    m_prev = [m_sc[h] for h in heads]
    m_new = [jnp.maximum(m_prev[h], jnp.max(s[h], axis=-1, keepdims=True)) for h in heads]
    alpha = [jnp.exp2(m_prev[h] - m_new[h]) for h in heads]
    p_all = [jnp.exp2(s[h] - m_new[h]) for h in heads]
    p16 = [p.astype(BF16) for p in p_all]
    pv = [functools.reduce(jnp.add, [_dot(p16[h][:, p * page:(p + 1) * page], head_rows(v_refs[p], h))
                                     for p in range(pp)]) for h in heads]
    for h in heads:
        l_sc[h] = alpha[h] * l_sc[h] + jnp.sum(p_all[h], axis=-1, keepdims=True)
        acc_sc[h] = alpha[h] * acc_sc[h] + pv[h]
        m_sc[h] = m_new[h]

    @pl.when(c == pl.num_programs(1) - 1)
    def _():
        lam = _lambda(lam_ref, lam_init)
        outs = []
        for h in range(n_heads):
            slope = jnp.float32(2.0 ** (-8.0 * (h + 1) / n_heads) * LOG2E)
            pad = jnp.zeros((page - nq, dh), F32)
            kn = jnp.concatenate([kn_ref[:, h, :], pad], axis=0).astype(BF16)
            vn = jnp.concatenate([vn_ref[:, h, :], pad], axis=0).astype(BF16)
            s = _dot_nt(q_rows(h), kn)
            jj = lax.broadcasted_iota(jnp.int32, (1, page), 1)
            r = lax.broadcasted_iota(jnp.int32, (2 * nq, 1), 0)
            ii = jnp.where(r >= nq, r - nq, r)
            s = jnp.where(jj <= ii, s + slope * jj.astype(F32), NEG_INF)
            _, l_new, acc_new = _softmax_update(s, vn, m_sc[h], l_sc[h], acc_sc[h])
            outs.append(_diff_finish(acc_new, l_new, lam, g_ref[...], 1.0 - lam_init, nq))
        o_ref[0] = jnp.concatenate(outs, axis=1).astype(o_ref.dtype)


def _attn_decode(qa, k_new, v_new, cache_k, cache_v, layer, page_table, lam_params, subln_g,
                 *, n_heads, d_qk, lam_init):
    b, nq, w = qa.shape
    n_pages = page_table.shape[1]
    pp = math.gcd(n_pages, DECODE_PAGES)
    dh = cache_k.shape[3]
    page = cache_k.shape[2] // n_heads

    def page_spec(p):
        return pl.BlockSpec((None, None, page * n_heads, dh),
                            lambda bb, c, pt: (layer, pt[bb * n_pages + c * pp + p], 0, 0))

    def tok_spec():
        return pl.BlockSpec((1, nq, w), lambda bb, c, pt: (bb, 0, 0))

    def new_spec():
        return pl.BlockSpec((None, nq, n_heads, dh), lambda bb, c, pt: (bb, 0, 0, 0))

    grid_spec = pltpu.PrefetchScalarGridSpec(
        num_scalar_prefetch=1,
        grid=(b, n_pages // pp),
        in_specs=[tok_spec(), new_spec(), new_spec(),
                  pl.BlockSpec(lam_params.shape, lambda bb, c, pt: (0, 0)),
                  pl.BlockSpec(subln_g.shape, lambda bb, c, pt: (0, 0))]
                 + [page_spec(p) for p in range(pp)] + [page_spec(p) for p in range(pp)],
        out_specs=tok_spec(),
        scratch_shapes=[pltpu.VMEM((n_heads, 2 * nq, 1), F32), pltpu.VMEM((n_heads, 2 * nq, 1), F32),
                        pltpu.VMEM((n_heads, 2 * nq, w // n_heads), F32)],
    )
    return pl.pallas_call(
        functools.partial(_attn_decode_kernel, pp=pp, n_heads=n_heads, d_qk=d_qk, lam_init=lam_init,
                          past_len=n_pages * page),
        grid_spec=grid_spec,
        out_shape=jax.ShapeDtypeStruct((b, nq, w), BF16),
        compiler_params=_cparams(("parallel", "arbitrary")),
        name="attn_decode",
    )(page_table.reshape(-1), qa, k_new, v_new, lam_params, subln_g,
      *([cache_k] * pp), *([cache_v] * pp))


def _cumsum_rows(x):
    c = x.shape[0]
    tri = (lax.broadcasted_iota(jnp.int32, (c, c), 0) >= lax.broadcasted_iota(jnp.int32, (c, c), 1))
    tri = jnp.where(tri, 1.0, 0.0).astype(BF16)
    hi = x.astype(BF16)
    r1 = x - hi.astype(F32)
    mid = r1.astype(BF16)
    lo = (r1 - mid.astype(F32)).astype(BF16)
    return _dot(tri, hi) + _dot(tri, mid) + _dot(tri, lo)


def _hgrn_kernel(q_ref, k_ref, v_ref, lf_ref, sg_ref, g_ref, o_ref, s_ref, st_sc, *, n_heads, c):
    c_idx = pl.program_id(1)
    dk = q_ref.shape[2] // n_heads
    dv = v_ref.shape[2] // n_heads
    t_sub = min(SUB_CHUNK, c)
    n_sub = c // t_sub
    heads = range(n_heads)
    ks = [slice(h * dk, (h + 1) * dk) for h in heads]
    vs = [slice(h * dv, (h + 1) * dv) for h in heads]
    tt = lax.broadcasted_iota(jnp.int32, (c, c), 0)
    ss = lax.broadcasted_iota(jnp.int32, (c, c), 1)

    @pl.when(c_idx == 0)
    def _():
        st_sc[...] = jnp.zeros(st_sc.shape, F32)

    st = [st_sc[h] for h in heads]
    for ci in range(q_ref.shape[1] // c):
        rows = slice(ci * c, (ci + 1) * c)
        q = q_ref[0, rows, :].astype(F32)
        k = k_ref[0, rows, :].astype(F32)
        v16 = v_ref[0, rows, :]
        b = _cumsum_rows(lf_ref[0, rows, :])
        anchors = [b[i * t_sub + t_sub // 2 - 1:i * t_sub + t_sub // 2, :] for i in range(n_sub)]
        anchor_rows = jnp.concatenate([jnp.broadcast_to(a, (t_sub, a.shape[1])) for a in anchors], axis=0)
        q_intra = (q * jnp.exp(jnp.minimum(b - anchor_rows, EXP_CLAMP))).astype(BF16)
        q_inter = (q * jnp.exp(b)).astype(BF16)
        b_last = b[c - 1:c, :]
        k_state = (k * jnp.exp(b_last - b)).astype(BF16)
        decay = jnp.exp(b_last)
        k_parts = []
        for i, a in enumerate(anchors):
            hi = (i + 1) * t_sub
            k_parts.append((k[:hi] * jnp.exp(jnp.minimum(a - b[:hi], EXP_CLAMP))).astype(BF16))
            if hi < c:
                k_parts.append(jnp.zeros((c - hi, k.shape[1]), BF16))
        k_intra = jnp.concatenate(k_parts, axis=0)

        wide = [_dot_nt(q_intra[:, ks[h]], k_intra[:, ks[h]]) for h in heads]
        a_mat = [jnp.concatenate([w[i * t_sub:(i + 1) * t_sub, i * c:(i + 1) * c] for i in range(n_sub)],
                                 axis=0) for w in wide]
        a_mat = [jnp.where(ss <= tt, a, 0.0).astype(BF16) for a in a_mat]
        o = [_dot_nt(q_inter[:, ks[h]], st[h].astype(BF16)) + _dot(a_mat[h], v16[:, vs[h]]) for h in heads]
        st = [decay[:, ks[h]] * st[h] + _dot_tn(v16[:, vs[h]], k_state[:, ks[h]]) for h in heads]
        outs = [_rms(o[h], g_ref[...]) * sg_ref[0, rows, vs[h]].astype(F32) for h in heads]
        o_ref[0, rows, :] = jnp.concatenate(outs, axis=1).astype(o_ref.dtype)
    for h in heads:
        st_sc[h] = st[h]

    @pl.when(c_idx == pl.num_programs(1) - 1)
    def _():
        for h in range(n_heads):
            s_ref[0, h] = st_sc[h].T


def _hgrn_prompt(qb, kb, vb, lf, sg, norm_g, *, n_heads):
    b, l, wk = qb.shape
    wv = vb.shape[2]
    c = _pick_tile(l, HGRN_CHUNK)
    step = c * math.gcd(l // c, HGRN_STEP_CHUNKS)
    dk, dv = wk // n_heads, wv // n_heads

    def tok(width):
        return pl.BlockSpec((1, step, width), lambda bb, cc: (bb, cc, 0))

    return pl.pallas_call(
        functools.partial(_hgrn_kernel, n_heads=n_heads, c=c),
        grid=(b, l // step),
        in_specs=[tok(wk), tok(wk), tok(wv), tok(wk), tok(wv), pl.BlockSpec(norm_g.shape, lambda bb, cc: (0, 0))],
        out_specs=[tok(wv), pl.BlockSpec((1, n_heads, dk, dv), lambda bb, cc: (bb, 0, 0, 0))],
        out_shape=[jax.ShapeDtypeStruct((b, l, wv), BF16), jax.ShapeDtypeStruct((b, n_heads, dk, dv), F32)],
        scratch_shapes=[pltpu.VMEM((n_heads, dv, dk), F32)],
        compiler_params=_cparams(("parallel", "arbitrary")),
        name="hgrn_prompt",
    )(qb, kb, vb, lf, sg, norm_g)


def _hgrn_decode_kernel(q_ref, k_ref, v_ref, lf_ref, sg_ref, g_ref, s0_ref, o_ref, s_ref, *, n_heads):
    n = q_ref.shape[1]
    dk = q_ref.shape[2] // n_heads
    dv = v_ref.shape[2] // n_heads
    row = lax.broadcasted_iota(jnp.int32, (n, 1), 0)
    pad = 2 * 8 - n if n < 2 * 8 else 0
    outs = []
    for h in range(n_heads):
        ks = slice(h * dk, (h + 1) * dk)
        vs = slice(h * dv, (h + 1) * dv)
        q = q_ref[0][:, ks].astype(F32)
        k = k_ref[0][:, ks].astype(F32)
        v = v_ref[0][:, vs].astype(F32)
        lf = lf_ref[0][:, ks]
        st = s0_ref[0, h].T
        b = lf
        for sh in [1 << e for e in range((n - 1).bit_length())]:
            shifted = jnp.concatenate([jnp.zeros((sh, dk), F32), b[:n - sh]], axis=0)
            b = b + shifted

        def padded(x):
            return jnp.concatenate([x, jnp.zeros((pad, x.shape[1]), F32)], axis=0) if pad else x

        o = _dot_nt(padded(q * jnp.exp(b)).astype(BF16), st.astype(BF16))[:n]
        for s in range(n):
            w = jnp.sum(q * k[s:s + 1] * jnp.exp(jnp.minimum(b - b[s:s + 1], 0.0)), axis=-1, keepdims=True)
            o = o + jnp.where(row >= s, w, 0.0) * v[s:s + 1]
        b_last = b[n - 1:n]
        kd = padded(k * jnp.exp(b_last - b)).astype(BF16)
        st_new = jnp.exp(b_last) * st + _dot_tn(padded(v).astype(BF16), kd)
        s_ref[0, h] = st_new.T
        outs.append(_rms(o, g_ref[...]) * sg_ref[0][:, vs].astype(F32))
    o_ref[0] = jnp.concatenate(outs, axis=1).astype(o_ref.dtype)


def _hgrn_decode(qb, kb, vb, lf, sg, norm_g, state, layer, *, n_heads):
    b, n, wk = qb.shape
    wv = vb.shape[2]
    dk, dv = wk // n_heads, wv // n_heads

    def tok(width):
        return pl.BlockSpec((1, n, width), lambda bb: (bb, 0, 0))

    st = pl.BlockSpec((1, n_heads, dk, dv), lambda bb: (bb, 0, 0, 0))
    st_in = pl.BlockSpec((None, 1, n_heads, dk, dv), lambda bb: (layer, bb, 0, 0, 0))
    return pl.pallas_call(
        functools.partial(_hgrn_decode_kernel, n_heads=n_heads),
        grid=(b,),
        in_specs=[tok(wk), tok(wk), tok(wv), tok(wk), tok(wv), pl.BlockSpec(norm_g.shape, lambda bb: (0, 0)), st_in],
        out_specs=[tok(wv), st],
        out_shape=[jax.ShapeDtypeStruct((b, n, wv), BF16), jax.ShapeDtypeStruct((b, n_heads, dk, dv), F32)],
        compiler_params=_cparams(("parallel",)),
        name="hgrn_decode",
    )(qb, kb, vb, lf, sg, norm_g, state)


def _merge_kernel(h_ref, oa_ref, ob_ref, ga_ref, gb_ref, wa_ref, wb_ref, wo_ref, g_ref, o_ref):
    m = (jax.nn.sigmoid(ga_ref[...].astype(F32)) * _dot(oa_ref[...], wa_ref[...])
         + jax.nn.sigmoid(gb_ref[...].astype(F32)) * _dot(ob_ref[...], wb_ref[...]))
    o_ref[...] = h_ref[...] + _rms(_dot(m.astype(BF16), wo_ref[...]), g_ref[...])


def _merge(h, oa, ob, ga, gb, w_proj_a, w_proj_b, w_out, post_g):
    n, d = h.shape
    tm = _pick_tile(n, MERGE_ROWS)

    def row(width):
        return pl.BlockSpec((tm, width), lambda i: (i, 0))

    return pl.pallas_call(
        _merge_kernel,
        grid=(n // tm,),
        in_specs=[row(d), row(oa.shape[1]), row(ob.shape[1]), row(d), row(d),
                  _resident(w_proj_a.shape), _resident(w_proj_b.shape), _resident(w_out.shape), _resident((1, d))],
        out_specs=row(d),
        out_shape=jax.ShapeDtypeStruct((n, d), F32),
        compiler_params=_cparams(("parallel",)),
        name="merge",
    )(h, oa, ob, ga, gb, w_proj_a, w_proj_b, w_out, post_g)


def kernel(x_prompt, x_sample, cache_k, cache_v, state_s, page_table, ffn1_pre_g, ffn1_post_g, ffn1_w_gu, ffn1_w_down, mix_pre_g, mix_post_g, w_in, lambda_q1, lambda_k1, lambda_q2, lambda_k2, attn_subln_g, hgrn_lb_logits, hgrn_norm_g, w_proj_a, w_proj_b, w_out, ffn2_pre_g, ffn2_post_g, ffn2_w_gu, ffn2_w_down):
    depth, d_model = ffn1_pre_g.shape
    bp, lp, _ = x_prompt.shape
    bs, ls, _ = x_sample.shape
    n_heads_a, two_dqk = cache_k.shape[3], cache_k.shape[4]
    d_qk = two_dqk // 2
    d_v_a = cache_v.shape[4]
    n_heads_b, d_k_b, d_v_b = state_s.shape[2:]
    w_qk, w_a = n_heads_a * two_dqk, n_heads_a * d_v_a
    w_kb, w_b = n_heads_b * d_k_b, n_heads_b * d_v_b
    widths = (w_qk, w_qk, w_a, w_kb, w_kb, w_b, w_b, d_model, d_model)
    assert sum(widths) == w_in.shape[2] and two_dqk == d_v_a == LANES and cache_k.shape[2] == PAGE_SIZE

    cache_k2 = cache_k.reshape(depth, cache_k.shape[1], PAGE_SIZE * n_heads_a, two_dqk)
    cache_v2 = cache_v.reshape(depth, cache_v.shape[1], PAGE_SIZE * n_heads_a, d_v_a)
    yp = x_prompt.reshape(bp * lp, d_model)
    ys = x_sample.reshape(bs * ls, d_model)
    kp_l, vp_l, sp_l, ks_l, vs_l, ss_l = [], [], [], [], [], []
    for l in range(depth):
        lam_init = 0.8 - 0.6 * math.exp(-0.3 * l)
        row = lambda g: g[l].reshape(1, -1)
        wgu1, wdn1 = ffn1_w_gu[l].astype(BF16), ffn1_w_down[l].astype(BF16)
        wgu2, wdn2 = ffn2_w_gu[l].astype(BF16), ffn2_w_down[l].astype(BF16)
        win = w_in[l].astype(BF16)
        wpa, wpb, wo = w_proj_a[l].astype(BF16), w_proj_b[l].astype(BF16), w_out[l].astype(BF16)
        lam_params = jnp.stack([lambda_q1[l], lambda_k1[l], lambda_q2[l], lambda_k2[l]])
        subln_g, norm_g = row(attn_subln_g), row(hgrn_norm_g)

        def mixer_inputs(y, seq_len):
            h = _ffn(y, row(ffn1_pre_g), row(ffn1_post_g), wgu1, wdn1)
            return h, _proj(h, row(mix_pre_g), win, hgrn_lb_logits, widths, l, d_qk ** -0.5 * LOG2E,
                            n_heads_a, seq_len)

        def mixer_outputs(h, oa, ob, ga, gb):
            h2 = _merge(h, oa, ob, ga, gb, wpa, wpb, wo, row(mix_post_g))
            return _ffn(h2, row(ffn2_pre_g), row(ffn2_post_g), wgu2, wdn2)

        h, (qa, ka, ka16, va, vat, qb, lf, kb, vb, sg, ga, gb) = mixer_inputs(yp, lp)
        seq = lambda a: a.reshape(bp, lp, a.shape[1])
        oa = _attn_prompt(seq(qa), seq(ka16), vat, lam_params, subln_g,
                          n_heads=n_heads_a, d_qk=d_qk, lam_init=lam_init)
        ob, sp = _hgrn_prompt(seq(qb), seq(kb), seq(vb), seq(lf), seq(sg), norm_g, n_heads=n_heads_b)
        yp = mixer_outputs(h, oa.reshape(bp * lp, w_a), ob.reshape(bp * lp, w_b), ga, gb)
        kp_l.append(ka.reshape(bp, lp, n_heads_a, two_dqk))
        vp_l.append(va.reshape(bp, lp, n_heads_a, d_v_a))
        sp_l.append(sp)

        h, (qa, ka, _, va, _, qb, lf, kb, vb, sg, ga, gb) = mixer_inputs(ys, ls)
        seq = lambda a: a.reshape(bs, ls, a.shape[1])
        ka_s = ka.reshape(bs, ls, n_heads_a, two_dqk)
        va_s = va.reshape(bs, ls, n_heads_a, d_v_a)
        oa = _attn_decode(seq(qa), ka_s, va_s, cache_k2, cache_v2, l, page_table, lam_params, subln_g,
                          n_heads=n_heads_a, d_qk=d_qk, lam_init=lam_init)
        ob, ss = _hgrn_decode(seq(qb), seq(kb), seq(vb), seq(lf), seq(sg), norm_g, state_s, l, n_heads=n_heads_b)
        ys = mixer_outputs(h, oa.reshape(bs * ls, w_a), ob.reshape(bs * ls, w_b), ga, gb)
        ks_l.append(ka_s)
        vs_l.append(va_s)
        ss_l.append(ss)

    def layers(xs):
        return xs[0][None] if len(xs) == 1 else jnp.stack(xs)

    return (yp.reshape(bp, lp, d_model), ys.reshape(bs, ls, d_model),
            layers(kp_l), layers(vp_l), layers(sp_l), layers(ks_l), layers(vs_l), layers(ss_l))
```

```python
import functools
import math

import numpy as np
import jax
import jax.numpy as jnp
from jax import lax
from jax.experimental import pallas as pl
from jax.experimental.pallas import tpu as pltpu

F32 = jnp.float32
BF16 = jnp.bfloat16
EPS = 1e-6
NEG_INF = -1e30
LOG2E = math.log2(math.e)
PAGE_SIZE = 128
LANES = 128
SUB_CHUNK = 16
EXP_CLAMP = 80.0
SUM_ROWS = 16
N_FEAT = 9
VMEM_LIMIT = 56 * 1024 * 1024
FFN_ROWS = 1024
PROJ_ROWS = 512
MERGE_ROWS = 1024
ATTN_Q_ROWS = 512
ATTN_KV_ROWS = 512
ATTN_HEADS_PER_STEP = 4
HGRN_CHUNK = 128
HGRN_DECODE_SEQS = 1
HGRN_STEP_CHUNKS = 4
DECODE_PAGES = 32


def _cparams(sem):
    return pltpu.CompilerParams(dimension_semantics=sem, vmem_limit_bytes=VMEM_LIMIT)


def _resident(shape):
    nd = len(shape)
    return pl.BlockSpec(shape, lambda *_: (0,) * nd, pipeline_mode=pl.Buffered(1))


def _rms(x, g):
    return x * lax.rsqrt(jnp.mean(x * x, axis=-1, keepdims=True) + EPS) * g


def _dot(a, b):
    return jnp.dot(a, b, preferred_element_type=F32)


def _dot_nt(a, b):
    return lax.dot_general(a, b, (((1,), (1,)), ((), ())), preferred_element_type=F32)


def _dot_tn(a, b):
    return lax.dot_general(a, b, (((0,), (0,)), ((), ())), preferred_element_type=F32)


def _pick_tile(n, target):
    t = math.gcd(n, target)
    assert t % 8 == 0 or t == n, (n, target)
    return t


def _ffn_kernel(x_ref, pre_ref, post_ref, wgu_ref, wdn_ref, o_ref, *, d_ff, ck):
    x = x_ref[...]
    xn = _rms(x, pre_ref[...]).astype(BF16)
    acc = None
    for c in range(d_ff // ck):
        g = _dot(xn, wgu_ref[:, c * ck:(c + 1) * ck])
        u = _dot(xn, wgu_ref[:, d_ff + c * ck:d_ff + (c + 1) * ck])
        a = (g * jax.nn.sigmoid(g) * u).astype(BF16)
        part = _dot(a, wdn_ref[c * ck:(c + 1) * ck, :])
        acc = part if acc is None else acc + part
    o_ref[...] = x + 0.5 * _rms(acc, post_ref[...])


def _ffn(x, pre_g, post_g, w_gu, w_down):
    n, d = x.shape
    d_ff = w_down.shape[0]
    tm = _pick_tile(n, FFN_ROWS)
    ck = 2 * LANES if d_ff % (2 * LANES) == 0 else d_ff
    row = pl.BlockSpec((tm, d), lambda i: (i, 0))
    return pl.pallas_call(
        functools.partial(_ffn_kernel, d_ff=d_ff, ck=ck),
        grid=(n // tm,),
        in_specs=[row, _resident((1, d)), _resident((1, d)), _resident(w_gu.shape), _resident(w_down.shape)],
        out_specs=row,
        out_shape=jax.ShapeDtypeStruct((n, d), F32),
        compiler_params=_cparams(("parallel",)),
        name="ffn",
    )(x, pre_g, post_g, w_gu, w_down)


def _proj_kernel(h_ref, g_ref, win_ref, lbl_ref,
                 qa_ref, ka_ref, ka16_ref, va_ref, vat_ref,
                 qb_ref, lf_ref, kb_ref, vb_ref, sg_ref, ga_ref, gb_ref, *, offs, layer, q_scale):
    u = _rms(h_ref[...], g_ref[...]).astype(BF16)

    def z(i):
        return _dot(u, win_ref[:, offs[i]:offs[i + 1]])

    def store_heads(ref, x):
        dh = ref.shape[1]
        nh = x.shape[1] // dh
        for hh in range(nh):
            ref[pl.ds(hh, x.shape[0], stride=nh), :] = x[:, hh * dh:(hh + 1) * dh]

    qa_ref[...] = (z(0) * q_scale).astype(BF16)
    k = z(1)
    store_heads(ka_ref, k)
    ka16_ref[...] = k.astype(BF16)
    v = z(2)
    store_heads(va_ref, v)
    vat_ref[0] = v.T.astype(BF16)
    qb = z(3)
    qb_ref[...] = (qb * jax.nn.sigmoid(qb)).astype(BF16)
    lbl = lbl_ref[...]
    e = jnp.exp(lbl - jnp.max(lbl, axis=0, keepdims=True))
    lb = jnp.sum(e[:layer + 1], axis=0, keepdims=True) / jnp.sum(e, axis=0, keepdims=True)
    fr = z(4)
    lf_ref[...] = jnp.log(lb + (1.0 - lb) * jax.nn.sigmoid(fr))
    kb_ref[...] = ((1.0 - lb) * jax.nn.sigmoid(-fr)).astype(BF16)
    vb_ref[...] = z(5).astype(BF16)
    gbr = z(6)
    sg_ref[...] = (gbr * jax.nn.sigmoid(gbr)).astype(BF16)
    ga_ref[...] = z(7).astype(BF16)
    gb_ref[...] = z(8).astype(BF16)


def _proj(h, mix_pre_g, w_in, lb_logits, widths, layer, q_scale, n_heads_a, seq_len):
    n, d = h.shape
    tm = _pick_tile(n, PROJ_ROWS)
    offs = tuple(int(o) for o in np.concatenate([[0], np.cumsum(widths)]))
    w = widths
    if seq_len % tm:
        seq_len = n
    nb = seq_len // tm

    def out(width, dtype):
        return pl.BlockSpec((tm, width), lambda i: (i, 0)), jax.ShapeDtypeStruct((n, width), dtype)

    def out_heads(width):
        dh = width // n_heads_a
        return (pl.BlockSpec((tm * n_heads_a, dh), lambda i: (i, 0)),
                jax.ShapeDtypeStruct((n * n_heads_a, dh), F32))

    def out_transposed(width):
        return (pl.BlockSpec((1, width, tm), lambda i: (i // nb, 0, i % nb)),
                jax.ShapeDtypeStruct((n // seq_len, width, seq_len), BF16))

    outs = [out(w[0], BF16), out_heads(w[1]), out(w[1], BF16), out_heads(w[2]), out_transposed(w[2]),
            out(w[3], BF16), out(w[4], F32), out(w[4], BF16), out(w[5], BF16), out(w[6], BF16),
            out(w[7], BF16), out(w[8], BF16)]
    return pl.pallas_call(
        functools.partial(_proj_kernel, offs=offs, layer=layer, q_scale=q_scale),
        grid=(n // tm,),
        in_specs=[pl.BlockSpec((tm, d), lambda i: (i, 0)), _resident((1, d)), _resident(w_in.shape),
                  _resident(lb_logits.shape)],
        out_specs=[o[0] for o in outs],
        out_shape=[o[1] for o in outs],
        compiler_params=_cparams(("parallel",)),
        name="proj",
    )(h, mix_pre_g, w_in, lb_logits)


def _alibi_slope(h, n_heads):
    slope = jnp.float32(0.0)
    for hh in range(n_heads):
        slope = jnp.where(h == hh, jnp.float32(2.0 ** (-8.0 * (hh + 1) / n_heads)), slope)
    return slope


def _lambda(lam_ref, lam_init):
    lp = lam_ref[...]
    l1 = jnp.exp(jnp.sum(lp[0:1] * lp[1:2], axis=-1, keepdims=True))
    l2 = jnp.exp(jnp.sum(lp[2:3] * lp[3:4], axis=-1, keepdims=True))
    return l1 - l2 + lam_init


def _split_maps(x, d_qk):
    lane = lax.broadcasted_iota(jnp.int32, x.shape, 1)
    zero = jnp.zeros_like(x)
    return jnp.where(lane < d_qk, x, zero), jnp.where(lane >= d_qk, x, zero)


def _softmax_update(s, v, m_prev, l_prev, acc_prev):
    m_new = jnp.maximum(m_prev, jnp.max(s, axis=-1, keepdims=True))
    alpha = jnp.exp2(m_prev - m_new)
    p = jnp.exp2(s - m_new)
    l_new = alpha * l_prev + jnp.sum(p, axis=-1, keepdims=True)
    acc_new = alpha * acc_prev + _dot(p.astype(BF16), v)
    return m_new, l_new, acc_new


def _diff_finish(acc, l, lam, g, out_scale, rows):
    o = acc[:rows] / l[:rows] - lam * (acc[rows:] / l[rows:])
    return _rms(o, g) * out_scale


def _bf16_split3(x):
    hi = x.astype(BF16).astype(F32)
    mid = (x - hi).astype(BF16).astype(F32)
    lo = (x - hi - mid).astype(BF16).astype(F32)
    return hi, mid, lo


def _attn_kernel(qi_ref, kj_ref, qn_ref, kn_ref, q0_ref, k0_ref, qx_ref, kx_ref, vt_ref, lam_ref, g_ref,
                 o_ref, s_sc, kf_sc, qf_sc, m_sc, acc_sc, *, tq, tk, hp, d_qk, n_heads, lam_init):
    t = pl.program_id(2)
    i, j = qi_ref[t], kj_ref[t]
    i_n, j_n = qn_ref[t], kn_ref[t]
    last_j = ((i + 1) * tq - 1) // tk
    dh = 2 * d_qk
    lane = lax.broadcasted_iota(jnp.int32, (1, dh), 1)
    lane_pos = lax.broadcasted_iota(jnp.int32, (1, tq), 1)

    def scores(q_ref, kx, hh, i_s, j_s, mp, masked):
        qm = jnp.concatenate([q_ref[0, :, hh * dh:(hh + 1) * dh], qf_sc[hh]], axis=1)
        s = _dot_nt(kx[mp], qm)
        if masked:
            kpos = j_s * tk + lax.broadcasted_iota(jnp.int32, (tk, 1), 0)
            qpos = i_s * tq + lane_pos
            s = jnp.where(kpos <= qpos, s, NEG_INF)
        return s

    def k_operands(k_ref, i_s, j_s):
        off = jnp.asarray(j_s * tk - i_s * tq, F32).astype(BF16)
        kf = jnp.where((lane >= 6) & (lane < N_FEAT), off, kf_sc[...])
        out = []
        for hh in range(hp):
            k1, k2 = _split_maps(k_ref[0, :, hh * dh:(hh + 1) * dh], d_qk)
            out.append([jnp.concatenate([km, kf], axis=1) for km in (k1, k2)])
        return out

    heads = [pl.program_id(1) * hp + hh for hh in range(hp)]
    chains = [(hh, mp) for hh in range(hp) for mp in range(2)]

    @pl.when(t == 0)
    def _():
        jrel = lax.broadcasted_iota(jnp.int32, (tk, dh), 0)
        jlo = (jrel & 255).astype(F32)
        jhi = (jrel - (jrel & 255)).astype(F32)
        kf_sc[...] = jnp.where(lane < 3, jlo, jnp.where(lane < 6, jhi, 0.0)).astype(BF16)
        for hh in range(hp):
            s3 = _bf16_split3(jnp.full((1, dh), _alibi_slope(heads[hh], n_heads) * LOG2E, F32))
            row = jnp.where(lane % 3 == 0, s3[0], jnp.where(lane % 3 == 1, s3[1], s3[2]))
            row = jnp.where(lane < N_FEAT, row, 0.0)
            qf_sc[hh] = jnp.broadcast_to(row, (tq, dh)).astype(BF16)
        kx = k_operands(k0_ref, 0, 0)
        for c, (hh, mp) in enumerate(chains):
            s_sc[c] = scores(q0_ref, kx[hh], hh, 0, 0, mp, True)

    @pl.when(j == 0)
    def _():
        m_sc[...] = jnp.full(m_sc.shape, NEG_INF, F32)
        acc_sc[...] = jnp.zeros(acc_sc.shape, F32)

    def step(next_masked):
        kx = k_operands(kx_ref, i_n, j_n)
        ones = jnp.ones((SUM_ROWS, tk), BF16)
        vt1 = [jnp.concatenate([vt_ref[0, hh * dh:(hh + 1) * dh, :], ones], axis=0) for hh in range(hp)]
        for c, (hh, mp) in enumerate(chains):
            s = s_sc[c]
            s_sc[c] = scores(qx_ref, kx[hh], hh, i_n, j_n, mp, next_masked)
            m_prev = m_sc[c]
            m_new = jnp.maximum(m_prev, jnp.max(s, axis=0, keepdims=True))
            alpha = jnp.exp2(m_prev - m_new)
            p = jnp.exp2(s - m_new).astype(BF16)
            acc_sc[c] = alpha * acc_sc[c] + _dot(vt1[hh], p)
            m_sc[c] = m_new

    next_masked = (j_n + 1) * tk - 1 > i_n * tq

    @pl.when(next_masked)
    def _():
        step(True)

    @pl.when(jnp.logical_not(next_masked))
    def _():
        step(False)

    @pl.when(j == last_j)
    def _():
        lam = _lambda(lam_ref, lam_init)
        outs = []
        for hh in range(hp):
            a1, a2 = acc_sc[2 * hh], acc_sc[2 * hh + 1]
            o_t = a1[:dh] / a1[dh:dh + 1] - lam * (a2[:dh] / a2[dh:dh + 1])
            outs.append(_rms(o_t.T, g_ref[...]) * (1.0 - lam_init))
        o_ref[0] = jnp.concatenate(outs, axis=1).astype(o_ref.dtype)


def _attn_prompt(qa, ka, va_t, lam_params, subln_g, *, n_heads, d_qk, lam_init):
    b, l, w = qa.shape
    dh = w // n_heads
    hp = math.gcd(n_heads, ATTN_HEADS_PER_STEP)
    tq = _pick_tile(l, ATTN_Q_ROWS)
    tk = _pick_tile(l, ATTN_KV_ROWS)
    pairs = [(i, j) for i in range(l // tq) for j in range(((i + 1) * tq - 1) // tk + 1)]
    offs = np.asarray([j * tk - i * tq for i, j in pairs], np.float32)
    assert np.array_equal(offs.astype(jnp.bfloat16).astype(np.float32), offs) and d_qk >= N_FEAT
    nxt = pairs[1:] + pairs[-1:]
    tables = [jnp.asarray(col, jnp.int32) for col in
              ([p[0] for p in pairs], [p[1] for p in pairs], [p[0] for p in nxt], [p[1] for p in nxt])]

    def blk(rows, index):
        return pl.BlockSpec((1, rows, hp * dh), index)

    grid_spec = pltpu.PrefetchScalarGridSpec(
        num_scalar_prefetch=4,
        grid=(b, n_heads // hp, len(pairs)),
        in_specs=[
            blk(tq, lambda bb, hg, t, qi, kj, qn, kn: (bb, 0, hg)),
            blk(tk, lambda bb, hg, t, qi, kj, qn, kn: (bb, 0, hg)),
            blk(tq, lambda bb, hg, t, qi, kj, qn, kn: (bb, qn[t], hg)),
            blk(tk, lambda bb, hg, t, qi, kj, qn, kn: (bb, kn[t], hg)),
            pl.BlockSpec((1, hp * dh, tk), lambda bb, hg, t, qi, kj, qn, kn: (bb, hg, kj[t])),
            pl.BlockSpec(lam_params.shape, lambda bb, hg, t, qi, kj, qn, kn: (0, 0)),
            pl.BlockSpec(subln_g.shape, lambda bb, hg, t, qi, kj, qn, kn: (0, 0)),
        ],
        out_specs=blk(tq, lambda bb, hg, t, qi, kj, qn, kn: (bb, qi[t], hg)),
        scratch_shapes=[pltpu.VMEM((hp * 2, tk, tq), F32), pltpu.VMEM((tk, dh), BF16),
                        pltpu.VMEM((hp, tq, dh), BF16), pltpu.VMEM((hp * 2, 1, tq), F32),
                        pltpu.VMEM((hp * 2, dh + SUM_ROWS, tq), F32)],
    )
    return pl.pallas_call(
        functools.partial(_attn_kernel, tq=tq, tk=tk, hp=hp, d_qk=d_qk, n_heads=n_heads,
                          lam_init=lam_init),
        grid_spec=grid_spec,
        out_shape=jax.ShapeDtypeStruct((b, l, w), BF16),
        compiler_params=_cparams(("parallel", "parallel", "arbitrary")),
        name="attn_prompt",
    )(*tables, qa, ka, qa, ka, va_t, lam_params, subln_g)


def _attn_decode_kernel(pt_ref, q_ref, kn_ref, vn_ref, lam_ref, g_ref, ck_hbm, cv_hbm,
                        o_ref, kbuf, vbuf, sem, m_sc, l_sc, acc_sc,
                        *, pp, n_pages, layer, n_heads, d_qk, lam_init, past_len):
    c = pl.program_id(1)
    n_chunks = pl.num_programs(1)
    g = pl.program_id(0) * n_chunks + c
    n_steps = pl.num_programs(0) * n_chunks
    nq = q_ref.shape[1]
    dh = q_ref.shape[2] // n_heads
    page = kbuf.shape[2] // n_heads
    slot = lax.rem(g, 2)

    def page_copies(step, slot_):
        first = (step // n_chunks) * n_pages + lax.rem(step, n_chunks) * pp
        out = []
        for p in range(pp):
            pid = pt_ref[first + p]
            out.append(pltpu.make_async_copy(ck_hbm.at[layer, pid], kbuf.at[slot_, p], sem.at[0, slot_]))
            out.append(pltpu.make_async_copy(cv_hbm.at[layer, pid], vbuf.at[slot_, p], sem.at[1, slot_]))
        return out

    @pl.when(g == 0)
    def _():
        for cp in page_copies(g, slot):
            cp.start()

    for cp in page_copies(g, slot):
        cp.wait()
    nxt = jnp.minimum(g + 1, n_steps - 1)
    for cp in page_copies(nxt, 1 - slot):
        cp.start()

    k_refs = [kbuf.at[slot, p] for p in range(pp)]
    v_refs = [vbuf.at[slot, p] for p in range(pp)]

    def head_rows(ref, h):
        return ref[pl.ds(h, page, stride=n_heads), :].astype(BF16)

    @pl.when(c == 0)
    def _():
        m_sc[...] = jnp.full(m_sc.shape, NEG_INF, F32)
        l_sc[...] = jnp.zeros(l_sc.shape, F32)
        acc_sc[...] = jnp.zeros(acc_sc.shape, F32)

    def q_rows(h):
        qh = q_ref[0][:, h * dh:(h + 1) * dh].astype(F32)
        q1, q2 = _split_maps(qh, d_qk)
        return jnp.concatenate([q1, q2], axis=0).astype(BF16)

    heads = range(n_heads)
    slopes = [jnp.float32(2.0 ** (-8.0 * (h + 1) / n_heads) * LOG2E) for h in heads]
    qh = [q_rows(h) for h in heads]
    rel = (c * (pp * page) - past_len + lax.broadcasted_iota(jnp.int32, (1, pp * page), 1)).astype(F32)
    s = [jnp.concatenate([_dot_nt(qh[h], head_rows(k_refs[p], h)) for p in range(pp)], axis=1) + slopes[h] * rel
         for h in heads]
    m_prev = [m_sc[h] for h in heads]
    m_new = [jnp.maximum(m_prev[h], jnp.max(s[h], axis=-1, keepdims=True)) for h in heads]
    alpha = [jnp.exp2(m_prev[h] - m_new[h]) for h in heads]
    p_all = [jnp.exp2(s[h] - m_new[h]) for h in heads]
    p16 = [p.astype(BF16) for p in p_all]
    pv = [functools.reduce(jnp.add, [_dot(p16[h][:, p * page:(p + 1) * page], head_rows(v_refs[p], h))
                                     for p in range(pp)]) for h in heads]
    for h in heads:
        l_sc[h] = alpha[h] * l_sc[h] + jnp.sum(p_all[h], axis=-1, keepdims=True)
        acc_sc[h] = alpha[h] * acc_sc[h] + pv[h]
        m_sc[h] = m_new[h]

    @pl.when(c == pl.num_programs(1) - 1)
    def _():
        lam = _lambda(lam_ref, lam_init)
        outs = []
        for h in range(n_heads):
            slope = jnp.float32(2.0 ** (-8.0 * (h + 1) / n_heads) * LOG2E)
            pad = jnp.zeros((page - nq, dh), F32)
            kn = jnp.concatenate([kn_ref[:, h, :], pad], axis=0).astype(BF16)
            vn = jnp.concatenate([vn_ref[:, h, :], pad], axis=0).astype(BF16)
            s = _dot_nt(q_rows(h), kn)
            jj = lax.broadcasted_iota(jnp.int32, (1, page), 1)
            r = lax.broadcasted_iota(jnp.int32, (2 * nq, 1), 0)
            ii = jnp.where(r >= nq, r - nq, r)
            s = jnp.where(jj <= ii, s + slope * jj.astype(F32), NEG_INF)
            _, l_new, acc_new = _softmax_update(s, vn, m_sc[h], l_sc[h], acc_sc[h])
            outs.append(_diff_finish(acc_new, l_new, lam, g_ref[...], 1.0 - lam_init, nq))
        o_ref[0] = jnp.concatenate(outs, axis=1).astype(o_ref.dtype)

    @pl.when(g == n_steps - 1)
    def _():
        for cp in page_copies(nxt, 1 - slot):
            cp.wait()


def _attn_decode(qa, k_new, v_new, cache_k, cache_v, layer, page_table, lam_params, subln_g,
                 *, n_heads, d_qk, lam_init):
    b, nq, w = qa.shape
    n_pages = page_table.shape[1]
    pp = math.gcd(n_pages, DECODE_PAGES)
    dh = cache_k.shape[3]
    page = cache_k.shape[2] // n_heads

    def tok_spec():
        return pl.BlockSpec((1, nq, w), lambda bb, c, pt: (bb, 0, 0))

    def new_spec():
        return pl.BlockSpec((None, nq, n_heads, dh), lambda bb, c, pt: (bb, 0, 0, 0))

    grid_spec = pltpu.PrefetchScalarGridSpec(
        num_scalar_prefetch=1,
        grid=(b, n_pages // pp),
        in_specs=[tok_spec(), new_spec(), new_spec(),
                  pl.BlockSpec(lam_params.shape, lambda bb, c, pt: (0, 0)),
                  pl.BlockSpec(subln_g.shape, lambda bb, c, pt: (0, 0)),
                  pl.BlockSpec(memory_space=pl.ANY), pl.BlockSpec(memory_space=pl.ANY)],
        out_specs=tok_spec(),
        scratch_shapes=[pltpu.VMEM((2, pp, page * n_heads, dh), F32),
                        pltpu.VMEM((2, pp, page * n_heads, dh), F32),
                        pltpu.SemaphoreType.DMA((2, 2)),
                        pltpu.VMEM((n_heads, 2 * nq, 1), F32), pltpu.VMEM((n_heads, 2 * nq, 1), F32),
                        pltpu.VMEM((n_heads, 2 * nq, w // n_heads), F32)],
    )
    return pl.pallas_call(
        functools.partial(_attn_decode_kernel, pp=pp, n_pages=n_pages, layer=layer, n_heads=n_heads,
                          d_qk=d_qk, lam_init=lam_init, past_len=n_pages * page),
        grid_spec=grid_spec,
        out_shape=jax.ShapeDtypeStruct((b, nq, w), BF16),
        compiler_params=_cparams(("arbitrary", "arbitrary")),
        name="attn_decode",
    )(page_table.reshape(-1), qa, k_new, v_new, lam_params, subln_g, cache_k, cache_v)


def _cumsum_rows(x):
    c = x.shape[0]
    tri = (lax.broadcasted_iota(jnp.int32, (c, c), 0) >= lax.broadcasted_iota(jnp.int32, (c, c), 1))
    tri = jnp.where(tri, 1.0, 0.0).astype(BF16)
    hi = x.astype(BF16)
    r1 = x - hi.astype(F32)
    mid = r1.astype(BF16)
    lo = (r1 - mid.astype(F32)).astype(BF16)
    return _dot(tri, hi) + _dot(tri, mid) + _dot(tri, lo)


def _hgrn_kernel(q_ref, k_ref, v_ref, lf_ref, sg_ref, g_ref, o_ref, s_ref, st_sc, *, n_heads, c):
    c_idx = pl.program_id(1)
    dk = q_ref.shape[2] // n_heads
    dv = v_ref.shape[2] // n_heads
    t_sub = min(SUB_CHUNK, c)
    n_sub = c // t_sub
    heads = range(n_heads)
    ks = [slice(h * dk, (h + 1) * dk) for h in heads]
    vs = [slice(h * dv, (h + 1) * dv) for h in heads]
    tt = lax.broadcasted_iota(jnp.int32, (c, c), 0)
    ss = lax.broadcasted_iota(jnp.int32, (c, c), 1)

    @pl.when(c_idx == 0)
    def _():
        st_sc[...] = jnp.zeros(st_sc.shape, F32)

    st = [st_sc[h] for h in heads]
    for ci in range(q_ref.shape[1] // c):
        rows = slice(ci * c, (ci + 1) * c)
        q = q_ref[0, rows, :].astype(F32)
        k = k_ref[0, rows, :].astype(F32)
        v16 = v_ref[0, rows, :]
        b = _cumsum_rows(lf_ref[0, rows, :])
        anchors = [b[i * t_sub + t_sub // 2 - 1:i * t_sub + t_sub // 2, :] for i in range(n_sub)]
        anchor_rows = jnp.concatenate([jnp.broadcast_to(a, (t_sub, a.shape[1])) for a in anchors], axis=0)
        q_intra = (q * jnp.exp(jnp.minimum(b - anchor_rows, EXP_CLAMP))).astype(BF16)
        q_inter = (q * jnp.exp(b)).astype(BF16)
        b_last = b[c - 1:c, :]
        k_state = (k * jnp.exp(b_last - b)).astype(BF16)
        decay = jnp.exp(b_last)
        k_parts = []
        for i, a in enumerate(anchors):
            hi = (i + 1) * t_sub
            k_parts.append((k[:hi] * jnp.exp(jnp.minimum(a - b[:hi], EXP_CLAMP))).astype(BF16))
            if hi < c:
                k_parts.append(jnp.zeros((c - hi, k.shape[1]), BF16))
        k_intra = jnp.concatenate(k_parts, axis=0)

        n_grp = 2 if n_sub % 2 == 0 else 1
        per = n_sub // n_grp
        a_mat = []
        for h in heads:
            blocks = []
            for g in range(n_grp):
                qg = q_intra[g * per * t_sub:(g + 1) * per * t_sub, ks[h]]
                w = _dot_nt(qg, k_intra[g * per * c:(g + 1) * per * c, ks[h]])
                blocks += [w[li * t_sub:(li + 1) * t_sub, li * c:(li + 1) * c] for li in range(per)]
            a_mat.append(jnp.concatenate(blocks, axis=0))
        a_mat = [jnp.where(ss <= tt, a, 0.0).astype(BF16) for a in a_mat]
        o = [_dot_nt(q_inter[:, ks[h]], st[h].astype(BF16)) + _dot(a_mat[h], v16[:, vs[h]]) for h in heads]
        st = [decay[:, ks[h]] * st[h] + _dot_tn(v16[:, vs[h]], k_state[:, ks[h]]) for h in heads]
        outs = [_rms(o[h], g_ref[...]) * sg_ref[0, rows, vs[h]].astype(F32) for h in heads]
        o_ref[0, rows, :] = jnp.concatenate(outs, axis=1).astype(o_ref.dtype)
    for h in heads:
        st_sc[h] = st[h]

    @pl.when(c_idx == pl.num_programs(1) - 1)
    def _():
        for h in range(n_heads):
            s_ref[0, h] = st_sc[h].T


def _hgrn_prompt(qb, kb, vb, lf, sg, norm_g, *, n_heads):
    b, l, wk = qb.shape
    wv = vb.shape[2]
    c = _pick_tile(l, HGRN_CHUNK)
    step = c * math.gcd(l // c, HGRN_STEP_CHUNKS)
    dk, dv = wk // n_heads, wv // n_heads

    def tok(width):
        return pl.BlockSpec((1, step, width), lambda bb, cc: (bb, cc, 0))

    return pl.pallas_call(
        functools.partial(_hgrn_kernel, n_heads=n_heads, c=c),
        grid=(b, l // step),
        in_specs=[tok(wk), tok(wk), tok(wv), tok(wk), tok(wv), pl.BlockSpec(norm_g.shape, lambda bb, cc: (0, 0))],
        out_specs=[tok(wv), pl.BlockSpec((1, n_heads, dk, dv), lambda bb, cc: (bb, 0, 0, 0))],
        out_shape=[jax.ShapeDtypeStruct((b, l, wv), BF16), jax.ShapeDtypeStruct((b, n_heads, dk, dv), F32)],
        scratch_shapes=[pltpu.VMEM((n_heads, dv, dk), F32)],
        compiler_params=_cparams(("parallel", "arbitrary")),
        name="hgrn_prompt",
    )(qb, kb, vb, lf, sg, norm_g)


def _hgrn_decode_kernel(q_ref, k_ref, v_ref, lf_ref, sg_ref, g_ref, s0_ref, o_ref, s_ref, *, n_heads):
    n = q_ref.shape[1]
    dk = q_ref.shape[2] // n_heads
    dv = v_ref.shape[2] // n_heads
    row = lax.broadcasted_iota(jnp.int32, (n, 1), 0)
    pad = 2 * 8 - n if n < 2 * 8 else 0
    for bi, h in [(bi, h) for bi in range(q_ref.shape[0]) for h in range(n_heads)]:
        ks = slice(h * dk, (h + 1) * dk)
        vs = slice(h * dv, (h + 1) * dv)
        q = q_ref[bi, :, ks].astype(F32)
        k = k_ref[bi, :, ks].astype(F32)
        v = v_ref[bi, :, vs].astype(F32)
        lf = lf_ref[bi, :, ks]
        st = s0_ref[bi, h].T
        b = lf
        for sh in [1 << e for e in range((n - 1).bit_length())]:
            shifted = jnp.concatenate([jnp.zeros((sh, dk), F32), b[:n - sh]], axis=0)
            b = b + shifted

        def padded(x):
            return jnp.concatenate([x, jnp.zeros((pad, x.shape[1]), F32)], axis=0) if pad else x

        o = _dot_nt(padded(q * jnp.exp(b)).astype(BF16), st.astype(BF16))[:n]
        for s in range(n):
            w = jnp.sum(q * k[s:s + 1] * jnp.exp(jnp.minimum(b - b[s:s + 1], 0.0)), axis=-1, keepdims=True)
            o = o + jnp.where(row >= s, w, 0.0) * v[s:s + 1]
        b_last = b[n - 1:n]
        kd = padded(k * jnp.exp(b_last - b)).astype(BF16)
        st_new = jnp.exp(b_last) * st + _dot_tn(padded(v).astype(BF16), kd)
        s_ref[bi, h] = st_new.T
        o_ref[bi, :, vs] = (_rms(o, g_ref[...]) * sg_ref[bi, :, vs].astype(F32)).astype(o_ref.dtype)


def _hgrn_decode(qb, kb, vb, lf, sg, norm_g, state, layer, *, n_heads):
    b, n, wk = qb.shape
    wv = vb.shape[2]
    dk, dv = wk // n_heads, wv // n_heads

    nb = math.gcd(b, HGRN_DECODE_SEQS)

    def tok(width):
        return pl.BlockSpec((nb, n, width), lambda bb: (bb, 0, 0))

    st = pl.BlockSpec((nb, n_heads, dk, dv), lambda bb: (bb, 0, 0, 0))
    st_in = pl.BlockSpec((None, nb, n_heads, dk, dv), lambda bb: (layer, bb, 0, 0, 0))
    return pl.pallas_call(
        functools.partial(_hgrn_decode_kernel, n_heads=n_heads),
        grid=(b // nb,),
        in_specs=[tok(wk), tok(wk), tok(wv), tok(wk), tok(wv), pl.BlockSpec(norm_g.shape, lambda bb: (0, 0)), st_in],
        out_specs=[tok(wv), st],
        out_shape=[jax.ShapeDtypeStruct((b, n, wv), BF16), jax.ShapeDtypeStruct((b, n_heads, dk, dv), F32)],
        compiler_params=_cparams(("parallel",)),
        name="hgrn_decode",
    )(qb, kb, vb, lf, sg, norm_g, state)


def _merge_kernel(h_ref, oa_ref, ob_ref, ga_ref, gb_ref, wa_ref, wb_ref, wo_ref, g_ref, o_ref):
    m = (jax.nn.sigmoid(ga_ref[...].astype(F32)) * _dot(oa_ref[...], wa_ref[...])
         + jax.nn.sigmoid(gb_ref[...].astype(F32)) * _dot(ob_ref[...], wb_ref[...]))
    o_ref[...] = h_ref[...] + _rms(_dot(m.astype(BF16), wo_ref[...]), g_ref[...])


def _merge(h, oa, ob, ga, gb, w_proj_a, w_proj_b, w_out, post_g):
    n, d = h.shape
    tm = _pick_tile(n, MERGE_ROWS)

    def row(width):
        return pl.BlockSpec((tm, width), lambda i: (i, 0))

    return pl.pallas_call(
        _merge_kernel,
        grid=(n // tm,),
        in_specs=[row(d), row(oa.shape[1]), row(ob.shape[1]), row(d), row(d),
                  _resident(w_proj_a.shape), _resident(w_proj_b.shape), _resident(w_out.shape), _resident((1, d))],
        out_specs=row(d),
        out_shape=jax.ShapeDtypeStruct((n, d), F32),
        compiler_params=_cparams(("parallel",)),
        name="merge",
    )(h, oa, ob, ga, gb, w_proj_a, w_proj_b, w_out, post_g)


def kernel(x_prompt, x_sample, cache_k, cache_v, state_s, page_table, ffn1_pre_g, ffn1_post_g, ffn1_w_gu, ffn1_w_down, mix_pre_g, mix_post_g, w_in, lambda_q1, lambda_k1, lambda_q2, lambda_k2, attn_subln_g, hgrn_lb_logits, hgrn_norm_g, w_proj_a, w_proj_b, w_out, ffn2_pre_g, ffn2_post_g, ffn2_w_gu, ffn2_w_down):
    depth, d_model = ffn1_pre_g.shape
    bp, lp, _ = x_prompt.shape
    bs, ls, _ = x_sample.shape
    n_heads_a, two_dqk = cache_k.shape[3], cache_k.shape[4]
    d_qk = two_dqk // 2
    d_v_a = cache_v.shape[4]
    n_heads_b, d_k_b, d_v_b = state_s.shape[2:]
    w_qk, w_a = n_heads_a * two_dqk, n_heads_a * d_v_a
    w_kb, w_b = n_heads_b * d_k_b, n_heads_b * d_v_b
    widths = (w_qk, w_qk, w_a, w_kb, w_kb, w_b, w_b, d_model, d_model)
    assert sum(widths) == w_in.shape[2] and two_dqk == d_v_a == LANES and cache_k.shape[2] == PAGE_SIZE

    cache_k2 = cache_k.reshape(depth, cache_k.shape[1], PAGE_SIZE * n_heads_a, two_dqk)
    cache_v2 = cache_v.reshape(depth, cache_v.shape[1], PAGE_SIZE * n_heads_a, d_v_a)
    yp = x_prompt.reshape(bp * lp, d_model)
    ys = x_sample.reshape(bs * ls, d_model)
    kp_l, vp_l, sp_l, ks_l, vs_l, ss_l = [], [], [], [], [], []
    for l in range(depth):
        lam_init = 0.8 - 0.6 * math.exp(-0.3 * l)
        row = lambda g: g[l].reshape(1, -1)
        wgu1, wdn1 = ffn1_w_gu[l].astype(BF16), ffn1_w_down[l].astype(BF16)
        wgu2, wdn2 = ffn2_w_gu[l].astype(BF16), ffn2_w_down[l].astype(BF16)
        win = w_in[l].astype(BF16)
        wpa, wpb, wo = w_proj_a[l].astype(BF16), w_proj_b[l].astype(BF16), w_out[l].astype(BF16)
        lam_params = jnp.stack([lambda_q1[l], lambda_k1[l], lambda_q2[l], lambda_k2[l]])
        subln_g, norm_g = row(attn_subln_g), row(hgrn_norm_g)

        def mixer_inputs(y, seq_len):
            h = _ffn(y, row(ffn1_pre_g), row(ffn1_post_g), wgu1, wdn1)
            return h, _proj(h, row(mix_pre_g), win, hgrn_lb_logits, widths, l, d_qk ** -0.5 * LOG2E,
                            n_heads_a, seq_len)

        def mixer_outputs(h, oa, ob, ga, gb):
            h2 = _merge(h, oa, ob, ga, gb, wpa, wpb, wo, row(mix_post_g))
            return _ffn(h2, row(ffn2_pre_g), row(ffn2_post_g), wgu2, wdn2)

        h, (qa, ka, ka16, va, vat, qb, lf, kb, vb, sg, ga, gb) = mixer_inputs(yp, lp)
        seq = lambda a: a.reshape(bp, lp, a.shape[1])
        oa = _attn_prompt(seq(qa), seq(ka16), vat, lam_params, subln_g,
                          n_heads=n_heads_a, d_qk=d_qk, lam_init=lam_init)
        ob, sp = _hgrn_prompt(seq(qb), seq(kb), seq(vb), seq(lf), seq(sg), norm_g, n_heads=n_heads_b)
        yp = mixer_outputs(h, oa.reshape(bp * lp, w_a), ob.reshape(bp * lp, w_b), ga, gb)
        kp_l.append(ka.reshape(bp, lp, n_heads_a, two_dqk))
        vp_l.append(va.reshape(bp, lp, n_heads_a, d_v_a))
        sp_l.append(sp)

        h, (qa, ka, _, va, _, qb, lf, kb, vb, sg, ga, gb) = mixer_inputs(ys, ls)
        seq = lambda a: a.reshape(bs, ls, a.shape[1])
        ka_s = ka.reshape(bs, ls, n_heads_a, two_dqk)
        va_s = va.reshape(bs, ls, n_heads_a, d_v_a)
        oa = _attn_decode(seq(qa), ka_s, va_s, cache_k2, cache_v2, l, page_table, lam_params, subln_g,
                          n_heads=n_heads_a, d_qk=d_qk, lam_init=lam_init)
        ob, ss = _hgrn_decode(seq(qb), seq(kb), seq(vb), seq(lf), seq(sg), norm_g, state_s, l, n_heads=n_heads_b)
        ys = mixer_outputs(h, oa.reshape(bs * ls, w_a), ob.reshape(bs * ls, w_b), ga, gb)
        ks_l.append(ka_s)
        vs_l.append(va_s)
        ss_l.append(ss)

    def layers(xs):
        return xs[0][None] if len(xs) == 1 else jnp.stack(xs)

    return (yp.reshape(bp, lp, d_model), ys.reshape(bs, ls, d_model),
            layers(kp_l), layers(vp_l), layers(sp_l), layers(ks_l), layers(vs_l), layers(ss_l))
```

```python
import functools
import math

import numpy as np
import jax
import jax.numpy as jnp
from jax import lax
from jax.experimental import pallas as pl
from jax.experimental.pallas import tpu as pltpu

F32 = jnp.float32
BF16 = jnp.bfloat16
EPS = 1e-6
NEG_INF = -1e30
LOG2E = math.log2(math.e)
PAGE_SIZE = 128
LANES = 128
SUB_CHUNK = 16
EXP_CLAMP = 80.0
SUM_ROWS = 16
N_FEAT = 9
VMEM_LIMIT = 56 * 1024 * 1024
FFN_ROWS = 1024
FFN_COLS = 256
PROJ_ROWS = 512
MERGE_ROWS = 1024
ATTN_Q_ROWS = 512
ATTN_KV_ROWS = 512
ATTN_HEADS_PER_STEP = 4
HGRN_CHUNK = 128
HGRN_DECODE_SEQS = 1
HGRN_STEP_CHUNKS = 8
DECODE_PAGES = 32


def _cparams(sem):
    return pltpu.CompilerParams(dimension_semantics=sem, vmem_limit_bytes=VMEM_LIMIT)


def _resident(shape):
    nd = len(shape)
    return pl.BlockSpec(shape, lambda *_: (0,) * nd, pipeline_mode=pl.Buffered(1))


def _rms(x, g):
    return x * lax.rsqrt(jnp.mean(x * x, axis=-1, keepdims=True) + EPS) * g


def _dot(a, b):
    return jnp.dot(a, b, preferred_element_type=F32)


def _dot_nt(a, b):
    return lax.dot_general(a, b, (((1,), (1,)), ((), ())), preferred_element_type=F32)


def _dot_tn(a, b):
    return lax.dot_general(a, b, (((0,), (0,)), ((), ())), preferred_element_type=F32)


def _pick_tile(n, target):
    t = math.gcd(n, target)
    assert t % 8 == 0 or t == n, (n, target)
    return t


def _ffn_kernel(x_ref, pre_ref, post_ref, wgu_ref, wdn_ref, o_ref, *, d_ff, ck):
    x = x_ref[...]
    xn = _rms(x, pre_ref[...]).astype(BF16)
    acc = None
    for c in range(d_ff // ck):
        g = _dot(xn, wgu_ref[:, c * ck:(c + 1) * ck])
        u = _dot(xn, wgu_ref[:, d_ff + c * ck:d_ff + (c + 1) * ck])
        a = (g * jax.nn.sigmoid(g) * u).astype(BF16)
        part = _dot(a, wdn_ref[c * ck:(c + 1) * ck, :])
        acc = part if acc is None else acc + part
    o_ref[...] = x + 0.5 * _rms(acc, post_ref[...])


def _ffn(x, pre_g, post_g, w_gu, w_down):
    n, d = x.shape
    d_ff = w_down.shape[0]
    tm = _pick_tile(n, FFN_ROWS)
    ck = FFN_COLS if d_ff % FFN_COLS == 0 else d_ff
    row = pl.BlockSpec((tm, d), lambda i: (i, 0))
    return pl.pallas_call(
        functools.partial(_ffn_kernel, d_ff=d_ff, ck=ck),
        grid=(n // tm,),
        in_specs=[row, _resident((1, d)), _resident((1, d)), _resident(w_gu.shape), _resident(w_down.shape)],
        out_specs=row,
        out_shape=jax.ShapeDtypeStruct((n, d), F32),
        compiler_params=_cparams(("parallel",)),
        name="ffn",
    )(x, pre_g, post_g, w_gu, w_down)


def _proj_kernel(h_ref, g_ref, win_ref, lbl_ref,
                 qa_ref, ka_ref, ka16_ref, va_ref, vat_ref,
                 qb_ref, lf_ref, kb_ref, vb_ref, sg_ref, ga_ref, gb_ref, *, offs, layer, q_scale):
    u = _rms(h_ref[...], g_ref[...]).astype(BF16)

    def z(i):
        return _dot(u, win_ref[:, offs[i]:offs[i + 1]])

    def store_heads(ref, x):
        dh = ref.shape[1]
        nh = x.shape[1] // dh
        for hh in range(nh):
            ref[pl.ds(hh, x.shape[0], stride=nh), :] = x[:, hh * dh:(hh + 1) * dh]

    qa_ref[...] = (z(0) * q_scale).astype(BF16)
    k = z(1)
    store_heads(ka_ref, k)
    ka16_ref[...] = k.astype(BF16)
    v = z(2)
    store_heads(va_ref, v)
    vat_ref[0] = v.T.astype(BF16)
    qb = z(3)
    qb_ref[...] = (qb * jax.nn.sigmoid(qb)).astype(BF16)
    lbl = lbl_ref[...]
    e = jnp.exp(lbl - jnp.max(lbl, axis=0, keepdims=True))
    lb = jnp.sum(e[:layer + 1], axis=0, keepdims=True) / jnp.sum(e, axis=0, keepdims=True)
    fr = z(4)
    lf_ref[...] = jnp.log(lb + (1.0 - lb) * jax.nn.sigmoid(fr))
    kb_ref[...] = ((1.0 - lb) * jax.nn.sigmoid(-fr)).astype(BF16)
    vb_ref[...] = z(5).astype(BF16)
    gbr = z(6)
    sg_ref[...] = (gbr * jax.nn.sigmoid(gbr)).astype(BF16)
    ga_ref[...] = z(7).astype(BF16)
    gb_ref[...] = z(8).astype(BF16)


def _proj(h, mix_pre_g, w_in, lb_logits, widths, layer, q_scale, n_heads_a, seq_len):
    n, d = h.shape
    tm = _pick_tile(n, PROJ_ROWS)
    offs = tuple(int(o) for o in np.concatenate([[0], np.cumsum(widths)]))
    w = widths
    if seq_len % tm:
        seq_len = n
    nb = seq_len // tm

    def out(width, dtype):
        return pl.BlockSpec((tm, width), lambda i: (i, 0)), jax.ShapeDtypeStruct((n, width), dtype)

    def out_heads(width):
        dh = width // n_heads_a
        return (pl.BlockSpec((tm * n_heads_a, dh), lambda i: (i, 0)),
                jax.ShapeDtypeStruct((n * n_heads_a, dh), F32))

    def out_transposed(width):
        return (pl.BlockSpec((1, width, tm), lambda i: (i // nb, 0, i % nb)),
                jax.ShapeDtypeStruct((n // seq_len, width, seq_len), BF16))

    outs = [out(w[0], BF16), out_heads(w[1]), out(w[1], BF16), out_heads(w[2]), out_transposed(w[2]),
            out(w[3], BF16), out(w[4], F32), out(w[4], BF16), out(w[5], BF16), out(w[6], BF16),
            out(w[7], BF16), out(w[8], BF16)]
    return pl.pallas_call(
        functools.partial(_proj_kernel, offs=offs, layer=layer, q_scale=q_scale),
        grid=(n // tm,),
        in_specs=[pl.BlockSpec((tm, d), lambda i: (i, 0)), _resident((1, d)), _resident(w_in.shape),
                  _resident(lb_logits.shape)],
        out_specs=[o[0] for o in outs],
        out_shape=[o[1] for o in outs],
        compiler_params=_cparams(("parallel",)),
        name="proj",
    )(h, mix_pre_g, w_in, lb_logits)


def _alibi_slope(h, n_heads):
    slope = jnp.float32(0.0)
    for hh in range(n_heads):
        slope = jnp.where(h == hh, jnp.float32(2.0 ** (-8.0 * (hh + 1) / n_heads)), slope)
    return slope


def _lambda(lam_ref, lam_init):
    lp = lam_ref[...]
    l1 = jnp.exp(jnp.sum(lp[0:1] * lp[1:2], axis=-1, keepdims=True))
    l2 = jnp.exp(jnp.sum(lp[2:3] * lp[3:4], axis=-1, keepdims=True))
    return l1 - l2 + lam_init


def _split_maps(x, d_qk):
    lane = lax.broadcasted_iota(jnp.int32, x.shape, 1)
    zero = jnp.zeros_like(x)
    return jnp.where(lane < d_qk, x, zero), jnp.where(lane >= d_qk, x, zero)


def _softmax_update(s, v, m_prev, l_prev, acc_prev):
    m_new = jnp.maximum(m_prev, jnp.max(s, axis=-1, keepdims=True))
    alpha = jnp.exp2(m_prev - m_new)
    p = jnp.exp2(s - m_new)
    l_new = alpha * l_prev + jnp.sum(p, axis=-1, keepdims=True)
    acc_new = alpha * acc_prev + _dot(p.astype(BF16), v)
    return m_new, l_new, acc_new


def _diff_finish(acc, l, lam, g, out_scale, rows):
    o = acc[:rows] / l[:rows] - lam * (acc[rows:] / l[rows:])
    return _rms(o, g) * out_scale


def _bf16_split3(x):
    hi = x.astype(BF16).astype(F32)
    mid = (x - hi).astype(BF16).astype(F32)
    lo = (x - hi - mid).astype(BF16).astype(F32)
    return hi, mid, lo


def _attn_kernel(qi_ref, kj_ref, qn_ref, kn_ref, q0_ref, k0_ref, qx_ref, kx_ref, vt_ref, lam_ref, g_ref,
                 o_ref, s_sc, kf_sc, qf_sc, m_sc, acc_sc, *, tq, tk, hp, d_qk, n_heads, lam_init):
    t = pl.program_id(2)
    i, j = qi_ref[t], kj_ref[t]
    i_n, j_n = qn_ref[t], kn_ref[t]
    last_j = ((i + 1) * tq - 1) // tk
    dh = 2 * d_qk
    lane = lax.broadcasted_iota(jnp.int32, (1, dh), 1)
    lane_pos = lax.broadcasted_iota(jnp.int32, (1, tq), 1)

    def scores(q_ref, kx, hh, i_s, j_s, mp, masked):
        qm = jnp.concatenate([q_ref[0, :, hh * dh:(hh + 1) * dh], qf_sc[hh]], axis=1)
        s = _dot_nt(kx[mp], qm)
        if masked:
            kpos = j_s * tk + lax.broadcasted_iota(jnp.int32, (tk, 1), 0)
            qpos = i_s * tq + lane_pos
            s = jnp.where(kpos <= qpos, s, NEG_INF)
        return s

    def k_operands(k_ref, i_s, j_s):
        off = jnp.asarray(j_s * tk - i_s * tq, F32).astype(BF16)
        kf = jnp.where((lane >= 6) & (lane < N_FEAT), off, kf_sc[...])
        out = []
        for hh in range(hp):
            k1, k2 = _split_maps(k_ref[0, :, hh * dh:(hh + 1) * dh], d_qk)
            out.append([jnp.concatenate([km, kf], axis=1) for km in (k1, k2)])
        return out

    heads = [pl.program_id(1) * hp + hh for hh in range(hp)]
    chains = [(hh, mp) for hh in range(hp) for mp in range(2)]

    @pl.when(t == 0)
    def _():
        jrel = lax.broadcasted_iota(jnp.int32, (tk, dh), 0)
        jlo = (jrel & 255).astype(F32)
        jhi = (jrel - (jrel & 255)).astype(F32)
        kf_sc[...] = jnp.where(lane < 3, jlo, jnp.where(lane < 6, jhi, 0.0)).astype(BF16)
        for hh in range(hp):
            s3 = _bf16_split3(jnp.full((1, dh), _alibi_slope(heads[hh], n_heads) * LOG2E, F32))
            row = jnp.where(lane % 3 == 0, s3[0], jnp.where(lane % 3 == 1, s3[1], s3[2]))
            row = jnp.where(lane < N_FEAT, row, 0.0)
            qf_sc[hh] = jnp.broadcast_to(row, (tq, dh)).astype(BF16)
        kx = k_operands(k0_ref, 0, 0)
        for c, (hh, mp) in enumerate(chains):
            s_sc[c] = scores(q0_ref, kx[hh], hh, 0, 0, mp, True)

    @pl.when(j == 0)
    def _():
        m_sc[...] = jnp.full(m_sc.shape, NEG_INF, F32)
        acc_sc[...] = jnp.zeros(acc_sc.shape, F32)

    def step(next_masked):
        kx = k_operands(kx_ref, i_n, j_n)
        ones = jnp.ones((SUM_ROWS, tk), BF16)
        vt1 = [jnp.concatenate([vt_ref[0, hh * dh:(hh + 1) * dh, :], ones], axis=0) for hh in range(hp)]
        for c, (hh, mp) in enumerate(chains):
            s = s_sc[c]
            s_sc[c] = scores(qx_ref, kx[hh], hh, i_n, j_n, mp, next_masked)
            m_prev = m_sc[c]
            m_new = jnp.maximum(m_prev, jnp.max(s, axis=0, keepdims=True))
            alpha = jnp.exp2(m_prev - m_new)
            p = jnp.exp2(s - m_new).astype(BF16)
            acc_sc[c] = alpha * acc_sc[c] + _dot(vt1[hh], p)
            m_sc[c] = m_new

    next_masked = (j_n + 1) * tk - 1 > i_n * tq

    @pl.when(next_masked)
    def _():
        step(True)

    @pl.when(jnp.logical_not(next_masked))
    def _():
        step(False)

    @pl.when(j == last_j)
    def _():
        lam = _lambda(lam_ref, lam_init)
        outs = []
        for hh in range(hp):
            a1, a2 = acc_sc[2 * hh], acc_sc[2 * hh + 1]
            o_t = a1[:dh] / a1[dh:dh + 1] - lam * (a2[:dh] / a2[dh:dh + 1])
            outs.append(_rms(o_t.T, g_ref[...]) * (1.0 - lam_init))
        o_ref[0] = jnp.concatenate(outs, axis=1).astype(o_ref.dtype)


def _attn_prompt(qa, ka, va_t, lam_params, subln_g, *, n_heads, d_qk, lam_init):
    b, l, w = qa.shape
    dh = w // n_heads
    hp = math.gcd(n_heads, ATTN_HEADS_PER_STEP)
    tq = _pick_tile(l, ATTN_Q_ROWS)
    tk = _pick_tile(l, ATTN_KV_ROWS)
    pairs = [(i, j) for i in range(l // tq) for j in range(((i + 1) * tq - 1) // tk + 1)]
    offs = np.asarray([j * tk - i * tq for i, j in pairs], np.float32)
    assert np.array_equal(offs.astype(jnp.bfloat16).astype(np.float32), offs) and d_qk >= N_FEAT
    nxt = pairs[1:] + pairs[-1:]
    tables = [jnp.asarray(col, jnp.int32) for col in
              ([p[0] for p in pairs], [p[1] for p in pairs], [p[0] for p in nxt], [p[1] for p in nxt])]

    def blk(rows, index):
        return pl.BlockSpec((1, rows, hp * dh), index)

    grid_spec = pltpu.PrefetchScalarGridSpec(
        num_scalar_prefetch=4,
        grid=(b, n_heads // hp, len(pairs)),
        in_specs=[
            blk(tq, lambda bb, hg, t, qi, kj, qn, kn: (bb, 0, hg)),
            blk(tk, lambda bb, hg, t, qi, kj, qn, kn: (bb, 0, hg)),
            blk(tq, lambda bb, hg, t, qi, kj, qn, kn: (bb, qn[t], hg)),
            blk(tk, lambda bb, hg, t, qi, kj, qn, kn: (bb, kn[t], hg)),
            pl.BlockSpec((1, hp * dh, tk), lambda bb, hg, t, qi, kj, qn, kn: (bb, hg, kj[t])),
            pl.BlockSpec(lam_params.shape, lambda bb, hg, t, qi, kj, qn, kn: (0, 0)),
            pl.BlockSpec(subln_g.shape, lambda bb, hg, t, qi, kj, qn, kn: (0, 0)),
        ],
        out_specs=blk(tq, lambda bb, hg, t, qi, kj, qn, kn: (bb, qi[t], hg)),
        scratch_shapes=[pltpu.VMEM((hp * 2, tk, tq), F32), pltpu.VMEM((tk, dh), BF16),
                        pltpu.VMEM((hp, tq, dh), BF16), pltpu.VMEM((hp * 2, 1, tq), F32),
                        pltpu.VMEM((hp * 2, dh + SUM_ROWS, tq), F32)],
    )
    return pl.pallas_call(
        functools.partial(_attn_kernel, tq=tq, tk=tk, hp=hp, d_qk=d_qk, n_heads=n_heads,
                          lam_init=lam_init),
        grid_spec=grid_spec,
        out_shape=jax.ShapeDtypeStruct((b, l, w), BF16),
        compiler_params=_cparams(("parallel", "parallel", "arbitrary")),
        name="attn_prompt",
    )(*tables, qa, ka, qa, ka, va_t, lam_params, subln_g)


def _attn_decode_kernel(pt_ref, q_ref, kn_ref, vn_ref, lam_ref, g_ref, *rest,
                        pp, n_heads, d_qk, lam_init, past_len):
    k_refs = rest[:pp]
    v_refs = rest[pp:2 * pp]
    o_ref, m_sc, l_sc, acc_sc = rest[2 * pp:]
    c = pl.program_id(1)
    nq = q_ref.shape[1]
    dh = q_ref.shape[2] // n_heads
    page = k_refs[0].shape[0] // n_heads

    def head_rows(ref, h):
        return ref[pl.ds(h, page, stride=n_heads), :].astype(BF16)

    @pl.when(c == 0)
    def _():
        m_sc[...] = jnp.full(m_sc.shape, NEG_INF, F32)
        l_sc[...] = jnp.zeros(l_sc.shape, F32)
        acc_sc[...] = jnp.zeros(acc_sc.shape, F32)

    def q_rows(h):
        qh = q_ref[0][:, h * dh:(h + 1) * dh].astype(F32)
        q1, q2 = _split_maps(qh, d_qk)
        return jnp.concatenate([q1, q2], axis=0).astype(BF16)

    heads = range(n_heads)
    slopes = [jnp.float32(2.0 ** (-8.0 * (h + 1) / n_heads) * LOG2E) for h in heads]
    qh = [q_rows(h) for h in heads]
    rel = (c * (pp * page) - past_len + lax.broadcasted_iota(jnp.int32, (1, pp * page), 1)).astype(F32)
    s = [jnp.concatenate([_dot_nt(qh[h], head_rows(k_refs[p], h)) for p in range(pp)], axis=1) + slopes[h] * rel
         for h in heads]
    m_prev = [m_sc[h] for h in heads]
    m_new = [jnp.maximum(m_prev[h], jnp.max(s[h], axis=-1, keepdims=True)) for h in heads]
    alpha = [jnp.exp2(m_prev[h] - m_new[h]) for h in heads]
    p_all = [jnp.exp2(s[h] - m_new[h]) for h in heads]
    p16 = [p.astype(BF16) for p in p_all]
    pv = [functools.reduce(jnp.add, [_dot(p16[h][:, p * page:(p + 1) * page], head_rows(v_refs[p], h))
                                     for p in range(pp)]) for h in heads]
    for h in heads:
        l_sc[h] = alpha[h] * l_sc[h] + jnp.sum(p_all[h], axis=-1, keepdims=True)
        acc_sc[h] = alpha[h] * acc_sc[h] + pv[h]
        m_sc[h] = m_new[h]

    @pl.when(c == pl.num_programs(1) - 1)
    def _():
        lam = _lambda(lam_ref, lam_init)
        outs = []
        for h in range(n_heads):
            slope = jnp.float32(2.0 ** (-8.0 * (h + 1) / n_heads) * LOG2E)
            pad = jnp.zeros((page - nq, dh), F32)
            kn = jnp.concatenate([kn_ref[:, h, :], pad], axis=0).astype(BF16)
            vn = jnp.concatenate([vn_ref[:, h, :], pad], axis=0).astype(BF16)
            s = _dot_nt(q_rows(h), kn)
            jj = lax.broadcasted_iota(jnp.int32, (1, page), 1)
            r = lax.broadcasted_iota(jnp.int32, (2 * nq, 1), 0)
            ii = jnp.where(r >= nq, r - nq, r)
            s = jnp.where(jj <= ii, s + slope * jj.astype(F32), NEG_INF)
            _, l_new, acc_new = _softmax_update(s, vn, m_sc[h], l_sc[h], acc_sc[h])
            outs.append(_diff_finish(acc_new, l_new, lam, g_ref[...], 1.0 - lam_init, nq))
        o_ref[0] = jnp.concatenate(outs, axis=1).astype(o_ref.dtype)


def _attn_decode(qa, k_new, v_new, cache_k, cache_v, layer, page_table, lam_params, subln_g,
                 *, n_heads, d_qk, lam_init):
    b, nq, w = qa.shape
    n_pages = page_table.shape[1]
    pp = math.gcd(n_pages, DECODE_PAGES)
    dh = cache_k.shape[3]
    page = cache_k.shape[2] // n_heads

    def page_spec(p):
        return pl.BlockSpec((None, None, page * n_heads, dh),
                            lambda bb, c, pt: (layer, pt[bb * n_pages + c * pp + p], 0, 0))

    def tok_spec():
        return pl.BlockSpec((1, nq, w), lambda bb, c, pt: (bb, 0, 0))

    def new_spec():
        return pl.BlockSpec((None, nq, n_heads, dh), lambda bb, c, pt: (bb, 0, 0, 0))

    grid_spec = pltpu.PrefetchScalarGridSpec(
        num_scalar_prefetch=1,
        grid=(b, n_pages // pp),
        in_specs=[tok_spec(), new_spec(), new_spec(),
                  pl.BlockSpec(lam_params.shape, lambda bb, c, pt: (0, 0)),
                  pl.BlockSpec(subln_g.shape, lambda bb, c, pt: (0, 0))]
                 + [page_spec(p) for p in range(pp)] + [page_spec(p) for p in range(pp)],
        out_specs=tok_spec(),
        scratch_shapes=[pltpu.VMEM((n_heads, 2 * nq, 1), F32), pltpu.VMEM((n_heads, 2 * nq, 1), F32),
                        pltpu.VMEM((n_heads, 2 * nq, w // n_heads), F32)],
    )
    return pl.pallas_call(
        functools.partial(_attn_decode_kernel, pp=pp, n_heads=n_heads, d_qk=d_qk, lam_init=lam_init,
                          past_len=n_pages * page),
        grid_spec=grid_spec,
        out_shape=jax.ShapeDtypeStruct((b, nq, w), BF16),
        compiler_params=_cparams(("parallel", "arbitrary")),
        name="attn_decode",
    )(page_table.reshape(-1), qa, k_new, v_new, lam_params, subln_g,
      *([cache_k] * pp), *([cache_v] * pp))


def _cumsum_rows(x):
    c = x.shape[0]
    tri = (lax.broadcasted_iota(jnp.int32, (c, c), 0) >= lax.broadcasted_iota(jnp.int32, (c, c), 1))
    tri = jnp.where(tri, 1.0, 0.0).astype(BF16)
    hi = x.astype(BF16)
    r1 = x - hi.astype(F32)
    mid = r1.astype(BF16)
    lo = (r1 - mid.astype(F32)).astype(BF16)
    return _dot(tri, hi) + _dot(tri, mid) + _dot(tri, lo)


def _hgrn_kernel(q_ref, k_ref, v_ref, lf_ref, sg_ref, g_ref, o_ref, s_ref, st_sc, *, n_heads, c):
    c_idx = pl.program_id(1)
    dk = q_ref.shape[2] // n_heads
    dv = v_ref.shape[2] // n_heads
    t_sub = min(SUB_CHUNK, c)
    n_sub = c // t_sub
    heads = range(n_heads)
    ks = [slice(h * dk, (h + 1) * dk) for h in heads]
    vs = [slice(h * dv, (h + 1) * dv) for h in heads]
    tt = lax.broadcasted_iota(jnp.int32, (c, c), 0)
    ss = lax.broadcasted_iota(jnp.int32, (c, c), 1)

    @pl.when(c_idx == 0)
    def _():
        st_sc[...] = jnp.zeros(st_sc.shape, F32)

    st = [st_sc[h] for h in heads]
    for ci in range(q_ref.shape[1] // c):
        rows = slice(ci * c, (ci + 1) * c)
        q = q_ref[0, rows, :].astype(F32)
        k = k_ref[0, rows, :].astype(F32)
        v16 = v_ref[0, rows, :]
        b = _cumsum_rows(lf_ref[0, rows, :])
        anchors = [b[i * t_sub + t_sub // 2 - 1:i * t_sub + t_sub // 2, :] for i in range(n_sub)]
        anchor_rows = jnp.concatenate([jnp.broadcast_to(a, (t_sub, a.shape[1])) for a in anchors], axis=0)
        q_intra = (q * jnp.exp(jnp.minimum(b - anchor_rows, EXP_CLAMP))).astype(BF16)
        q_inter = (q * jnp.exp(b)).astype(BF16)
        b_last = b[c - 1:c, :]
        k_state = (k * jnp.exp(b_last - b)).astype(BF16)
        decay = jnp.exp(b_last)
        k_parts = []
        for i, a in enumerate(anchors):
            hi = (i + 1) * t_sub
            k_parts.append((k[:hi] * jnp.exp(jnp.minimum(a - b[:hi], EXP_CLAMP))).astype(BF16))
            if hi < c:
                k_parts.append(jnp.zeros((c - hi, k.shape[1]), BF16))
        k_intra = jnp.concatenate(k_parts, axis=0)

        n_grp = 2 if n_sub % 2 == 0 else 1
        per = n_sub // n_grp
        a_mat = []
        for h in heads:
            blocks = []
            for g in range(n_grp):
                qg = q_intra[g * per * t_sub:(g + 1) * per * t_sub, ks[h]]
                w = _dot_nt(qg, k_intra[g * per * c:(g + 1) * per * c, ks[h]])
                blocks += [w[li * t_sub:(li + 1) * t_sub, li * c:(li + 1) * c] for li in range(per)]
            a_mat.append(jnp.concatenate(blocks, axis=0))
        a_mat = [jnp.where(ss <= tt, a, 0.0).astype(BF16) for a in a_mat]
        o = [_dot_nt(q_inter[:, ks[h]], st[h].astype(BF16)) + _dot(a_mat[h], v16[:, vs[h]]) for h in heads]
        st = [decay[:, ks[h]] * st[h] + _dot_tn(v16[:, vs[h]], k_state[:, ks[h]]) for h in heads]
        outs = [_rms(o[h], g_ref[...]) * sg_ref[0, rows, vs[h]].astype(F32) for h in heads]
        o_ref[0, rows, :] = jnp.concatenate(outs, axis=1).astype(o_ref.dtype)
    for h in heads:
        st_sc[h] = st[h]

    @pl.when(c_idx == pl.num_programs(1) - 1)
    def _():
        for h in range(n_heads):
            s_ref[0, h] = st_sc[h].T


def _hgrn_prompt(qb, kb, vb, lf, sg, norm_g, *, n_heads):
    b, l, wk = qb.shape
    wv = vb.shape[2]
    c = _pick_tile(l, HGRN_CHUNK)
    step = c * math.gcd(l // c, HGRN_STEP_CHUNKS)
    dk, dv = wk // n_heads, wv // n_heads

    def tok(width):
        return pl.BlockSpec((1, step, width), lambda bb, cc: (bb, cc, 0))

    return pl.pallas_call(
        functools.partial(_hgrn_kernel, n_heads=n_heads, c=c),
        grid=(b, l // step),
        in_specs=[tok(wk), tok(wk), tok(wv), tok(wk), tok(wv), pl.BlockSpec(norm_g.shape, lambda bb, cc: (0, 0))],
        out_specs=[tok(wv), pl.BlockSpec((1, n_heads, dk, dv), lambda bb, cc: (bb, 0, 0, 0))],
        out_shape=[jax.ShapeDtypeStruct((b, l, wv), BF16), jax.ShapeDtypeStruct((b, n_heads, dk, dv), F32)],
        scratch_shapes=[pltpu.VMEM((n_heads, dv, dk), F32)],
        compiler_params=_cparams(("parallel", "arbitrary")),
        name="hgrn_prompt",
    )(qb, kb, vb, lf, sg, norm_g)


def _hgrn_decode_kernel(q_ref, k_ref, v_ref, lf_ref, sg_ref, g_ref, s0_ref, o_ref, s_ref, *, n_heads):
    n = q_ref.shape[1]
    dk = q_ref.shape[2] // n_heads
    dv = v_ref.shape[2] // n_heads
    row = lax.broadcasted_iota(jnp.int32, (n, 1), 0)
    pad = 2 * 8 - n if n < 2 * 8 else 0
    for bi, h in [(bi, h) for bi in range(q_ref.shape[0]) for h in range(n_heads)]:
        ks = slice(h * dk, (h + 1) * dk)
        vs = slice(h * dv, (h + 1) * dv)
        q = q_ref[bi, :, ks].astype(F32)
        k = k_ref[bi, :, ks].astype(F32)
        v = v_ref[bi, :, vs].astype(F32)
        lf = lf_ref[bi, :, ks]
        st = s0_ref[bi, h].T
        b = lf
        for sh in [1 << e for e in range((n - 1).bit_length())]:
            shifted = jnp.concatenate([jnp.zeros((sh, dk), F32), b[:n - sh]], axis=0)
            b = b + shifted

        def padded(x):
            return jnp.concatenate([x, jnp.zeros((pad, x.shape[1]), F32)], axis=0) if pad else x

        o = _dot_nt(padded(q * jnp.exp(b)).astype(BF16), st.astype(BF16))[:n]
        for s in range(n):
            w = jnp.sum(q * k[s:s + 1] * jnp.exp(jnp.minimum(b - b[s:s + 1], 0.0)), axis=-1, keepdims=True)
            o = o + jnp.where(row >= s, w, 0.0) * v[s:s + 1]
        b_last = b[n - 1:n]
        kd = padded(k * jnp.exp(b_last - b)).astype(BF16)
        st_new = jnp.exp(b_last) * st + _dot_tn(padded(v).astype(BF16), kd)
        s_ref[bi, h] = st_new.T
        o_ref[bi, :, vs] = (_rms(o, g_ref[...]) * sg_ref[bi, :, vs].astype(F32)).astype(o_ref.dtype)


def _hgrn_decode(qb, kb, vb, lf, sg, norm_g, state, layer, *, n_heads):
    b, n, wk = qb.shape
    wv = vb.shape[2]
    dk, dv = wk // n_heads, wv // n_heads

    nb = math.gcd(b, HGRN_DECODE_SEQS)

    def tok(width):
        return pl.BlockSpec((nb, n, width), lambda bb: (bb, 0, 0))

    st = pl.BlockSpec((nb, n_heads, dk, dv), lambda bb: (bb, 0, 0, 0))
    st_in = pl.BlockSpec((None, nb, n_heads, dk, dv), lambda bb: (layer, bb, 0, 0, 0))
    return pl.pallas_call(
        functools.partial(_hgrn_decode_kernel, n_heads=n_heads),
        grid=(b // nb,),
        in_specs=[tok(wk), tok(wk), tok(wv), tok(wk), tok(wv), pl.BlockSpec(norm_g.shape, lambda bb: (0, 0)), st_in],
        out_specs=[tok(wv), st],
        out_shape=[jax.ShapeDtypeStruct((b, n, wv), BF16), jax.ShapeDtypeStruct((b, n_heads, dk, dv), F32)],
        compiler_params=_cparams(("parallel",)),
        name="hgrn_decode",
    )(qb, kb, vb, lf, sg, norm_g, state)


def _merge_kernel(h_ref, oa_ref, ob_ref, ga_ref, gb_ref, wa_ref, wb_ref, wo_ref, g_ref, o_ref):
    m = (jax.nn.sigmoid(ga_ref[...].astype(F32)) * _dot(oa_ref[...], wa_ref[...])
         + jax.nn.sigmoid(gb_ref[...].astype(F32)) * _dot(ob_ref[...], wb_ref[...]))
    o_ref[...] = h_ref[...] + _rms(_dot(m.astype(BF16), wo_ref[...]), g_ref[...])


def _merge(h, oa, ob, ga, gb, w_proj_a, w_proj_b, w_out, post_g):
    n, d = h.shape
    tm = _pick_tile(n, MERGE_ROWS)

    def row(width):
        return pl.BlockSpec((tm, width), lambda i: (i, 0))

    return pl.pallas_call(
        _merge_kernel,
        grid=(n // tm,),
        in_specs=[row(d), row(oa.shape[1]), row(ob.shape[1]), row(d), row(d),
                  _resident(w_proj_a.shape), _resident(w_proj_b.shape), _resident(w_out.shape), _resident((1, d))],
        out_specs=row(d),
        out_shape=jax.ShapeDtypeStruct((n, d), F32),
        compiler_params=_cparams(("parallel",)),
        name="merge",
    )(h, oa, ob, ga, gb, w_proj_a, w_proj_b, w_out, post_g)


def kernel(x_prompt, x_sample, cache_k, cache_v, state_s, page_table, ffn1_pre_g, ffn1_post_g, ffn1_w_gu, ffn1_w_down, mix_pre_g, mix_post_g, w_in, lambda_q1, lambda_k1, lambda_q2, lambda_k2, attn_subln_g, hgrn_lb_logits, hgrn_norm_g, w_proj_a, w_proj_b, w_out, ffn2_pre_g, ffn2_post_g, ffn2_w_gu, ffn2_w_down):
    depth, d_model = ffn1_pre_g.shape
    bp, lp, _ = x_prompt.shape
    bs, ls, _ = x_sample.shape
    n_heads_a, two_dqk = cache_k.shape[3], cache_k.shape[4]
    d_qk = two_dqk // 2
    d_v_a = cache_v.shape[4]
    n_heads_b, d_k_b, d_v_b = state_s.shape[2:]
    w_qk, w_a = n_heads_a * two_dqk, n_heads_a * d_v_a
    w_kb, w_b = n_heads_b * d_k_b, n_heads_b * d_v_b
    widths = (w_qk, w_qk, w_a, w_kb, w_kb, w_b, w_b, d_model, d_model)
    assert sum(widths) == w_in.shape[2] and two_dqk == d_v_a == LANES and cache_k.shape[2] == PAGE_SIZE

    cache_k2 = cache_k.reshape(depth, cache_k.shape[1], PAGE_SIZE * n_heads_a, two_dqk)
    cache_v2 = cache_v.reshape(depth, cache_v.shape[1], PAGE_SIZE * n_heads_a, d_v_a)
    yp = x_prompt.reshape(bp * lp, d_model)
    ys = x_sample.reshape(bs * ls, d_model)
    kp_l, vp_l, sp_l, ks_l, vs_l, ss_l = [], [], [], [], [], []
    for l in range(depth):
        lam_init = 0.8 - 0.6 * math.exp(-0.3 * l)
        row = lambda g: g[l].reshape(1, -1)
        wgu1, wdn1 = ffn1_w_gu[l].astype(BF16), ffn1_w_down[l].astype(BF16)
        wgu2, wdn2 = ffn2_w_gu[l].astype(BF16), ffn2_w_down[l].astype(BF16)
        win = w_in[l].astype(BF16)
        wpa, wpb, wo = w_proj_a[l].astype(BF16), w_proj_b[l].astype(BF16), w_out[l].astype(BF16)
        lam_params = jnp.stack([lambda_q1[l], lambda_k1[l], lambda_q2[l], lambda_k2[l]])
        subln_g, norm_g = row(attn_subln_g), row(hgrn_norm_g)

        def mixer_inputs(y, seq_len):
            h = _ffn(y, row(ffn1_pre_g), row(ffn1_post_g), wgu1, wdn1)
            return h, _proj(h, row(mix_pre_g), win, hgrn_lb_logits, widths, l, d_qk ** -0.5 * LOG2E,
                            n_heads_a, seq_len)

        def mixer_outputs(h, oa, ob, ga, gb):
            h2 = _merge(h, oa, ob, ga, gb, wpa, wpb, wo, row(mix_post_g))
            return _ffn(h2, row(ffn2_pre_g), row(ffn2_post_g), wgu2, wdn2)

        h, (qa, ka, ka16, va, vat, qb, lf, kb, vb, sg, ga, gb) = mixer_inputs(yp, lp)
        seq = lambda a: a.reshape(bp, lp, a.shape[1])
        oa = _attn_prompt(seq(qa), seq(ka16), vat, lam_params, subln_g,
                          n_heads=n_heads_a, d_qk=d_qk, lam_init=lam_init)
        ob, sp = _hgrn_prompt(seq(qb), seq(kb), seq(vb), seq(lf), seq(sg), norm_g, n_heads=n_heads_b)
        yp = mixer_outputs(h, oa.reshape(bp * lp, w_a), ob.reshape(bp * lp, w_b), ga, gb)
        kp_l.append(ka.reshape(bp, lp, n_heads_a, two_dqk))
        vp_l.append(va.reshape(bp, lp, n_heads_a, d_v_a))
        sp_l.append(sp)

        h, (qa, ka, _, va, _, qb, lf, kb, vb, sg, ga, gb) = mixer_inputs(ys, ls)
        seq = lambda a: a.reshape(bs, ls, a.shape[1])
        ka_s = ka.reshape(bs, ls, n_heads_a, two_dqk)
        va_s = va.reshape(bs, ls, n_heads_a, d_v_a)
        oa = _attn_decode(seq(qa), ka_s, va_s, cache_k2, cache_v2, l, page_table, lam_params, subln_g,
                          n_heads=n_heads_a, d_qk=d_qk, lam_init=lam_init)
        ob, ss = _hgrn_decode(seq(qb), seq(kb), seq(vb), seq(lf), seq(sg), norm_g, state_s, l, n_heads=n_heads_b)
        ys = mixer_outputs(h, oa.reshape(bs * ls, w_a), ob.reshape(bs * ls, w_b), ga, gb)
        ks_l.append(ka_s)
        vs_l.append(va_s)
        ss_l.append(ss)

    def layers(xs):
        return xs[0][None] if len(xs) == 1 else jnp.stack(xs)

    return (yp.reshape(bp, lp, d_model), ys.reshape(bs, ls, d_model),
            layers(kp_l), layers(vp_l), layers(sp_l), layers(ks_l), layers(vs_l), layers(ss_l))
```

```python
import functools
import math

import numpy as np
import jax
import jax.numpy as jnp
from jax import lax
from jax.experimental import pallas as pl
from jax.experimental.pallas import tpu as pltpu

F32 = jnp.float32
BF16 = jnp.bfloat16
EPS = 1e-6
NEG_INF = -1e30
LOG2E = math.log2(math.e)
PAGE_SIZE = 128
LANES = 128
SUB_CHUNK = 16
EXP_CLAMP = 80.0
SUM_ROWS = 16
N_FEAT = 9
VMEM_LIMIT = 56 * 1024 * 1024
FFN_ROWS = 1024
FFN_COLS = 256
PROJ_ROWS = 512
MERGE_ROWS = 1024
ATTN_Q_ROWS = 512
ATTN_KV_ROWS = 512
ATTN_HEADS_PER_STEP = 4
HGRN_CHUNK = 128
HGRN_DECODE_SEQS = 1
HGRN_STEP_CHUNKS = 8
DECODE_PAGES = 32


def _cparams(sem):
    return pltpu.CompilerParams(dimension_semantics=sem, vmem_limit_bytes=VMEM_LIMIT)


def _resident(shape):
    nd = len(shape)
    return pl.BlockSpec(shape, lambda *_: (0,) * nd, pipeline_mode=pl.Buffered(1))


def _rms(x, g):
    return x * lax.rsqrt(jnp.mean(x * x, axis=-1, keepdims=True) + EPS) * g


def _dot(a, b):
    return jnp.dot(a, b, preferred_element_type=F32)


def _dot_nt(a, b):
    return lax.dot_general(a, b, (((1,), (1,)), ((), ())), preferred_element_type=F32)


def _dot_tn(a, b):
    return lax.dot_general(a, b, (((0,), (0,)), ((), ())), preferred_element_type=F32)


def _pick_tile(n, target):
    t = math.gcd(n, target)
    assert t % 8 == 0 or t == n, (n, target)
    return t


def _ffn_kernel(x_ref, pre_ref, post_ref, wgu_ref, wdn_ref, o_ref, *, d_ff, ck):
    x = x_ref[...]
    xn = _rms(x, pre_ref[...]).astype(BF16)
    acc = None
    for c in range(d_ff // ck):
        g = _dot(xn, wgu_ref[:, c * ck:(c + 1) * ck])
        u = _dot(xn, wgu_ref[:, d_ff + c * ck:d_ff + (c + 1) * ck])
        a = (g * jax.nn.sigmoid(g) * u).astype(BF16)
        part = _dot(a, wdn_ref[c * ck:(c + 1) * ck, :])
        acc = part if acc is None else acc + part
    o_ref[...] = x + 0.5 * _rms(acc, post_ref[...])


def _ffn(x, pre_g, post_g, w_gu, w_down):
    n, d = x.shape
    d_ff = w_down.shape[0]
    tm = _pick_tile(n, FFN_ROWS)
    ck = FFN_COLS if d_ff % FFN_COLS == 0 else d_ff
    row = pl.BlockSpec((tm, d), lambda i: (i, 0))
    return pl.pallas_call(
        functools.partial(_ffn_kernel, d_ff=d_ff, ck=ck),
        grid=(n // tm,),
        in_specs=[row, _resident((1, d)), _resident((1, d)), _resident(w_gu.shape), _resident(w_down.shape)],
        out_specs=row,
        out_shape=jax.ShapeDtypeStruct((n, d), F32),
        compiler_params=_cparams(("parallel",)),
        name="ffn",
    )(x, pre_g, post_g, w_gu, w_down)


def _proj_kernel(h_ref, g_ref, win_ref, lbl_ref,
                 qa_ref, ka_ref, ka16_ref, va_ref, vat_ref,
                 qb_ref, lf_ref, kb_ref, vb_ref, sg_ref, ga_ref, gb_ref, *, offs, layer, q_scale):
    u = _rms(h_ref[...], g_ref[...]).astype(BF16)

    def z(i):
        return _dot(u, win_ref[:, offs[i]:offs[i + 1]])

    def store_heads(ref, x):
        dh = ref.shape[1]
        nh = x.shape[1] // dh
        for hh in range(nh):
            ref[pl.ds(hh, x.shape[0], stride=nh), :] = x[:, hh * dh:(hh + 1) * dh]

    qa_ref[...] = (z(0) * q_scale).astype(BF16)
    k = z(1)
    store_heads(ka_ref, k)
    ka16_ref[...] = k.astype(BF16)
    v = z(2)
    store_heads(va_ref, v)
    vat_ref[0] = v.T.astype(BF16)
    qb = z(3)
    qb_ref[...] = (qb * jax.nn.sigmoid(qb)).astype(BF16)
    lbl = lbl_ref[...]
    e = jnp.exp(lbl - jnp.max(lbl, axis=0, keepdims=True))
    lb = jnp.sum(e[:layer + 1], axis=0, keepdims=True) / jnp.sum(e, axis=0, keepdims=True)
    fr = z(4)
    lf_ref[...] = jnp.log(lb + (1.0 - lb) * jax.nn.sigmoid(fr))
    kb_ref[...] = ((1.0 - lb) * jax.nn.sigmoid(-fr)).astype(BF16)
    vb_ref[...] = z(5).astype(BF16)
    gbr = z(6)
    sg_ref[...] = (gbr * jax.nn.sigmoid(gbr)).astype(BF16)
    ga_ref[...] = z(7).astype(BF16)
    gb_ref[...] = z(8).astype(BF16)


def _proj(h, mix_pre_g, w_in, lb_logits, widths, layer, q_scale, n_heads_a, seq_len):
    n, d = h.shape
    tm = _pick_tile(n, PROJ_ROWS)
    offs = tuple(int(o) for o in np.concatenate([[0], np.cumsum(widths)]))
    w = widths
    if seq_len % tm:
        seq_len = n
    nb = seq_len // tm

    def out(width, dtype):
        return pl.BlockSpec((tm, width), lambda i: (i, 0)), jax.ShapeDtypeStruct((n, width), dtype)

    def out_heads(width):
        dh = width // n_heads_a
        return (pl.BlockSpec((tm * n_heads_a, dh), lambda i: (i, 0)),
                jax.ShapeDtypeStruct((n * n_heads_a, dh), F32))

    def out_transposed(width):
        return (pl.BlockSpec((1, width, tm), lambda i: (i // nb, 0, i % nb)),
                jax.ShapeDtypeStruct((n // seq_len, width, seq_len), BF16))

    outs = [out(w[0], BF16), out_heads(w[1]), out(w[1], BF16), out_heads(w[2]), out_transposed(w[2]),
            out(w[3], BF16), out(w[4], F32), out(w[4], BF16), out(w[5], BF16), out(w[6], BF16),
            out(w[7], BF16), out(w[8], BF16)]
    return pl.pallas_call(
        functools.partial(_proj_kernel, offs=offs, layer=layer, q_scale=q_scale),
        grid=(n // tm,),
        in_specs=[pl.BlockSpec((tm, d), lambda i: (i, 0)), _resident((1, d)), _resident(w_in.shape),
                  _resident(lb_logits.shape)],
        out_specs=[o[0] for o in outs],
        out_shape=[o[1] for o in outs],
        compiler_params=_cparams(("parallel",)),
        name="proj",
    )(h, mix_pre_g, w_in, lb_logits)


def _alibi_slope(h, n_heads):
    slope = jnp.float32(0.0)
    for hh in range(n_heads):
        slope = jnp.where(h == hh, jnp.float32(2.0 ** (-8.0 * (hh + 1) / n_heads)), slope)
    return slope


def _lambda(lam_ref, lam_init):
    lp = lam_ref[...]
    l1 = jnp.exp(jnp.sum(lp[0:1] * lp[1:2], axis=-1, keepdims=True))
    l2 = jnp.exp(jnp.sum(lp[2:3] * lp[3:4], axis=-1, keepdims=True))
    return l1 - l2 + lam_init


def _split_maps(x, d_qk):
    lane = lax.broadcasted_iota(jnp.int32, x.shape, 1)
    zero = jnp.zeros_like(x)
    return jnp.where(lane < d_qk, x, zero), jnp.where(lane >= d_qk, x, zero)


def _softmax_update(s, v, m_prev, l_prev, acc_prev):
    m_new = jnp.maximum(m_prev, jnp.max(s, axis=-1, keepdims=True))
    alpha = jnp.exp2(m_prev - m_new)
    p = jnp.exp2(s - m_new)
    l_new = alpha * l_prev + jnp.sum(p, axis=-1, keepdims=True)
    acc_new = alpha * acc_prev + _dot(p.astype(BF16), v)
    return m_new, l_new, acc_new


def _diff_finish(acc, l, lam, g, out_scale, rows):
    o = acc[:rows] / l[:rows] - lam * (acc[rows:] / l[rows:])
    return _rms(o, g) * out_scale


def _bf16_split3(x):
    hi = x.astype(BF16).astype(F32)
    mid = (x - hi).astype(BF16).astype(F32)
    lo = (x - hi - mid).astype(BF16).astype(F32)
    return hi, mid, lo


def _attn_kernel(qi_ref, kj_ref, qn_ref, kn_ref, q0_ref, k0_ref, qx_ref, kx_ref, vt_ref, lam_ref, g_ref,
                 o_ref, s_sc, kf_sc, qf_sc, m_sc, acc_sc, *, tq, tk, hp, d_qk, n_heads, lam_init):
    t = pl.program_id(2)
    i, j = qi_ref[t], kj_ref[t]
    i_n, j_n = qn_ref[t], kn_ref[t]
    last_j = ((i + 1) * tq - 1) // tk
    dh = 2 * d_qk
    lane = lax.broadcasted_iota(jnp.int32, (1, dh), 1)

    def scores(q_ref, kx, hh, mp, causal):
        qm = jnp.concatenate([q_ref[0, :, hh * dh:(hh + 1) * dh], qf_sc[hh]], axis=1)
        s = _dot_nt(kx[mp], qm)
        return s if causal is None else jnp.where(causal, s, NEG_INF)

    def causal_mask(i_s, j_s):
        kpos = j_s * tk + lax.broadcasted_iota(jnp.int32, (tk, tq), 0)
        qpos = i_s * tq + lax.broadcasted_iota(jnp.int32, (tk, tq), 1)
        return kpos <= qpos

    def k_operands(k_ref, i_s, j_s):
        off = jnp.asarray(j_s * tk - i_s * tq, F32).astype(BF16)
        kf = jnp.where((lane >= 6) & (lane < N_FEAT), off, kf_sc[...])
        out = []
        for hh in range(hp):
            k1, k2 = _split_maps(k_ref[0, :, hh * dh:(hh + 1) * dh], d_qk)
            out.append([jnp.concatenate([km, kf], axis=1) for km in (k1, k2)])
        return out

    heads = [pl.program_id(1) * hp + hh for hh in range(hp)]
    chains = [(hh, mp) for hh in range(hp) for mp in range(2)]

    @pl.when(t == 0)
    def _():
        jrel = lax.broadcasted_iota(jnp.int32, (tk, dh), 0)
        jlo = (jrel & 255).astype(F32)
        jhi = (jrel - (jrel & 255)).astype(F32)
        kf_sc[...] = jnp.where(lane < 3, jlo, jnp.where(lane < 6, jhi, 0.0)).astype(BF16)
        for hh in range(hp):
            s3 = _bf16_split3(jnp.full((1, dh), _alibi_slope(heads[hh], n_heads) * LOG2E, F32))
            row = jnp.where(lane % 3 == 0, s3[0], jnp.where(lane % 3 == 1, s3[1], s3[2]))
            row = jnp.where(lane < N_FEAT, row, 0.0)
            qf_sc[hh] = jnp.broadcast_to(row, (tq, dh)).astype(BF16)
        kx = k_operands(k0_ref, 0, 0)
        causal = causal_mask(0, 0)
        for c, (hh, mp) in enumerate(chains):
            s_sc[c] = scores(q0_ref, kx[hh], hh, mp, causal)

    @pl.when(j == 0)
    def _():
        m_sc[...] = jnp.full(m_sc.shape, NEG_INF, F32)
        acc_sc[...] = jnp.zeros(acc_sc.shape, F32)

    def step(next_masked):
        causal = causal_mask(i_n, j_n) if next_masked else None
        kx = k_operands(kx_ref, i_n, j_n)
        ones = jnp.ones((SUM_ROWS, tk), BF16)
        vt1 = [jnp.concatenate([vt_ref[0, hh * dh:(hh + 1) * dh, :], ones], axis=0) for hh in range(hp)]
        for c, (hh, mp) in enumerate(chains):
            s = s_sc[c]
            s_sc[c] = scores(qx_ref, kx[hh], hh, mp, causal)
            m_prev = m_sc[c]
            m_new = jnp.maximum(m_prev, jnp.max(s, axis=0, keepdims=True))
            alpha = jnp.exp2(m_prev - m_new)
            p = jnp.exp2(s - m_new).astype(BF16)
            acc_sc[c] = alpha * acc_sc[c] + _dot(vt1[hh], p)
            m_sc[c] = m_new

    next_masked = (j_n + 1) * tk - 1 > i_n * tq

    @pl.when(next_masked)
    def _():
        step(True)

    @pl.when(jnp.logical_not(next_masked))
    def _():
        step(False)

    @pl.when(j == last_j)
    def _():
        lam = _lambda(lam_ref, lam_init)
        outs = []
        for hh in range(hp):
            a1, a2 = acc_sc[2 * hh], acc_sc[2 * hh + 1]
            o_t = a1[:dh] / a1[dh:dh + 1] - lam * (a2[:dh] / a2[dh:dh + 1])
            o_n = o_t * lax.rsqrt(jnp.mean(o_t * o_t, axis=0, keepdims=True) + EPS)
            outs.append(o_n.T * (g_ref[...] * (1.0 - lam_init)))
        o_ref[0] = jnp.concatenate(outs, axis=1).astype(o_ref.dtype)


def _attn_prompt(qa, ka, va_t, lam_params, subln_g, *, n_heads, d_qk, lam_init):
    b, l, w = qa.shape
    dh = w // n_heads
    hp = math.gcd(n_heads, ATTN_HEADS_PER_STEP)
    tq = _pick_tile(l, ATTN_Q_ROWS)
    tk = _pick_tile(l, ATTN_KV_ROWS)
    pairs = [(i, j) for i in range(l // tq) for j in range(((i + 1) * tq - 1) // tk + 1)]
    offs = np.asarray([j * tk - i * tq for i, j in pairs], np.float32)
    assert np.array_equal(offs.astype(jnp.bfloat16).astype(np.float32), offs) and d_qk >= N_FEAT
    nxt = pairs[1:] + pairs[-1:]
    tables = [jnp.asarray(col, jnp.int32) for col in
              ([p[0] for p in pairs], [p[1] for p in pairs], [p[0] for p in nxt], [p[1] for p in nxt])]

    def blk(rows, index):
        return pl.BlockSpec((1, rows, hp * dh), index)

    grid_spec = pltpu.PrefetchScalarGridSpec(
        num_scalar_prefetch=4,
        grid=(b, n_heads // hp, len(pairs)),
        in_specs=[
            blk(tq, lambda bb, hg, t, qi, kj, qn, kn: (bb, 0, hg)),
            blk(tk, lambda bb, hg, t, qi, kj, qn, kn: (bb, 0, hg)),
            blk(tq, lambda bb, hg, t, qi, kj, qn, kn: (bb, qn[t], hg)),
            blk(tk, lambda bb, hg, t, qi, kj, qn, kn: (bb, kn[t], hg)),
            pl.BlockSpec((1, hp * dh, tk), lambda bb, hg, t, qi, kj, qn, kn: (bb, hg, kj[t])),
            pl.BlockSpec(lam_params.shape, lambda bb, hg, t, qi, kj, qn, kn: (0, 0)),
            pl.BlockSpec(subln_g.shape, lambda bb, hg, t, qi, kj, qn, kn: (0, 0)),
        ],
        out_specs=blk(tq, lambda bb, hg, t, qi, kj, qn, kn: (bb, qi[t], hg)),
        scratch_shapes=[pltpu.VMEM((hp * 2, tk, tq), F32), pltpu.VMEM((tk, dh), BF16),
                        pltpu.VMEM((hp, tq, dh), BF16), pltpu.VMEM((hp * 2, 1, tq), F32),
                        pltpu.VMEM((hp * 2, dh + SUM_ROWS, tq), F32)],
    )
    return pl.pallas_call(
        functools.partial(_attn_kernel, tq=tq, tk=tk, hp=hp, d_qk=d_qk, n_heads=n_heads,
                          lam_init=lam_init),
        grid_spec=grid_spec,
        out_shape=jax.ShapeDtypeStruct((b, l, w), BF16),
        compiler_params=_cparams(("parallel", "parallel", "arbitrary")),
        name="attn_prompt",
    )(*tables, qa, ka, qa, ka, va_t, lam_params, subln_g)


def _attn_decode_kernel(pt_ref, q_ref, kn_ref, vn_ref, lam_ref, g_ref, *rest,
                        pp, n_heads, d_qk, lam_init, past_len):
    k_refs = rest[:pp]
    v_refs = rest[pp:2 * pp]
    o_ref, m_sc, l_sc, acc_sc = rest[2 * pp:]
    c = pl.program_id(1)
    nq = q_ref.shape[1]
    dh = q_ref.shape[2] // n_heads
    page = k_refs[0].shape[0] // n_heads

    def head_rows(ref, h):
        return ref[pl.ds(h, page, stride=n_heads), :].astype(BF16)

    @pl.when(c == 0)
    def _():
        m_sc[...] = jnp.full(m_sc.shape, NEG_INF, F32)
        l_sc[...] = jnp.zeros(l_sc.shape, F32)
        acc_sc[...] = jnp.zeros(acc_sc.shape, F32)

    def q_rows(h):
        qh = q_ref[0][:, h * dh:(h + 1) * dh].astype(F32)
        q1, q2 = _split_maps(qh, d_qk)
        return jnp.concatenate([q1, q2], axis=0).astype(BF16)

    heads = range(n_heads)
    slopes = [jnp.float32(2.0 ** (-8.0 * (h + 1) / n_heads) * LOG2E) for h in heads]
    qh = [q_rows(h) for h in heads]
    rel = (c * (pp * page) - past_len + lax.broadcasted_iota(jnp.int32, (1, pp * page), 1)).astype(F32)
    s = [jnp.concatenate([_dot_nt(qh[h], head_rows(k_refs[p], h)) for p in range(pp)], axis=1) + slopes[h] * rel
         for h in heads]
    m_prev = [m_sc[h] for h in heads]
    m_new = [jnp.maximum(m_prev[h], jnp.max(s[h], axis=-1, keepdims=True)) for h in heads]
    alpha = [jnp.exp2(m_prev[h] - m_new[h]) for h in heads]
    p_all = [jnp.exp2(s[h] - m_new[h]) for h in heads]
    p16 = [p.astype(BF16) for p in p_all]
    pv = [functools.reduce(jnp.add, [_dot(p16[h][:, p * page:(p + 1) * page], head_rows(v_refs[p], h))
                                     for p in range(pp)]) for h in heads]
    for h in heads:
        l_sc[h] = alpha[h] * l_sc[h] + jnp.sum(p_all[h], axis=-1, keepdims=True)
        acc_sc[h] = alpha[h] * acc_sc[h] + pv[h]
        m_sc[h] = m_new[h]

    @pl.when(c == pl.num_programs(1) - 1)
    def _():
        lam = _lambda(lam_ref, lam_init)
        outs = []
        for h in range(n_heads):
            slope = jnp.float32(2.0 ** (-8.0 * (h + 1) / n_heads) * LOG2E)
            pad = jnp.zeros((page - nq, dh), F32)
            kn = jnp.concatenate([kn_ref[:, h, :], pad], axis=0).astype(BF16)
            vn = jnp.concatenate([vn_ref[:, h, :], pad], axis=0).astype(BF16)
            s = _dot_nt(q_rows(h), kn)
            jj = lax.broadcasted_iota(jnp.int32, (1, page), 1)
            r = lax.broadcasted_iota(jnp.int32, (2 * nq, 1), 0)
            ii = jnp.where(r >= nq, r - nq, r)
            s = jnp.where(jj <= ii, s + slope * jj.astype(F32), NEG_INF)
            _, l_new, acc_new = _softmax_update(s, vn, m_sc[h], l_sc[h], acc_sc[h])
            outs.append(_diff_finish(acc_new, l_new, lam, g_ref[...], 1.0 - lam_init, nq))
        o_ref[0] = jnp.concatenate(outs, axis=1).astype(o_ref.dtype)


def _attn_decode(qa, k_new, v_new, cache_k, cache_v, layer, page_table, lam_params, subln_g,
                 *, n_heads, d_qk, lam_init):
    b, nq, w = qa.shape
    n_pages = page_table.shape[1]
    pp = math.gcd(n_pages, DECODE_PAGES)
    dh = cache_k.shape[3]
    page = cache_k.shape[2] // n_heads

    def page_spec(p):
        return pl.BlockSpec((None, None, page * n_heads, dh),
                            lambda bb, c, pt: (layer, pt[bb * n_pages + c * pp + p], 0, 0))

    def tok_spec():
        return pl.BlockSpec((1, nq, w), lambda bb, c, pt: (bb, 0, 0))

    def new_spec():
        return pl.BlockSpec((None, nq, n_heads, dh), lambda bb, c, pt: (bb, 0, 0, 0))

    grid_spec = pltpu.PrefetchScalarGridSpec(
        num_scalar_prefetch=1,
        grid=(b, n_pages // pp),
        in_specs=[tok_spec(), new_spec(), new_spec(),
                  pl.BlockSpec(lam_params.shape, lambda bb, c, pt: (0, 0)),
                  pl.BlockSpec(subln_g.shape, lambda bb, c, pt: (0, 0))]
                 + [page_spec(p) for p in range(pp)] + [page_spec(p) for p in range(pp)],
        out_specs=tok_spec(),
        scratch_shapes=[pltpu.VMEM((n_heads, 2 * nq, 1), F32), pltpu.VMEM((n_heads, 2 * nq, 1), F32),
                        pltpu.VMEM((n_heads, 2 * nq, w // n_heads), F32)],
    )
    return pl.pallas_call(
        functools.partial(_attn_decode_kernel, pp=pp, n_heads=n_heads, d_qk=d_qk, lam_init=lam_init,
                          past_len=n_pages * page),
        grid_spec=grid_spec,
        out_shape=jax.ShapeDtypeStruct((b, nq, w), BF16),
        compiler_params=_cparams(("parallel", "arbitrary")),
        name="attn_decode",
    )(page_table.reshape(-1), qa, k_new, v_new, lam_params, subln_g,
      *([cache_k] * pp), *([cache_v] * pp))


def _cumsum_rows(x):
    c = x.shape[0]
    tri = (lax.broadcasted_iota(jnp.int32, (c, c), 0) >= lax.broadcasted_iota(jnp.int32, (c, c), 1))
    tri = jnp.where(tri, 1.0, 0.0).astype(BF16)
    hi = x.astype(BF16)
    r1 = x - hi.astype(F32)
    mid = r1.astype(BF16)
    lo = (r1 - mid.astype(F32)).astype(BF16)
    return _dot(tri, hi) + _dot(tri, mid) + _dot(tri, lo)


def _hgrn_kernel(q_ref, k_ref, v_ref, lf_ref, sg_ref, g_ref, o_ref, s_ref, st_sc, *, n_heads, c):
    c_idx = pl.program_id(1)
    dk = q_ref.shape[2] // n_heads
    dv = v_ref.shape[2] // n_heads
    t_sub = min(SUB_CHUNK, c)
    n_sub = c // t_sub
    heads = range(n_heads)
    ks = [slice(h * dk, (h + 1) * dk) for h in heads]
    vs = [slice(h * dv, (h + 1) * dv) for h in heads]
    tt = lax.broadcasted_iota(jnp.int32, (c, c), 0)
    ss = lax.broadcasted_iota(jnp.int32, (c, c), 1)

    @pl.when(c_idx == 0)
    def _():
        st_sc[...] = jnp.zeros(st_sc.shape, F32)

    st = [st_sc[h] for h in heads]
    for ci in range(q_ref.shape[1] // c):
        rows = slice(ci * c, (ci + 1) * c)
        q = q_ref[0, rows, :].astype(F32)
        k = k_ref[0, rows, :].astype(F32)
        v16 = v_ref[0, rows, :]
        b = _cumsum_rows(lf_ref[0, rows, :])
        anchors = [b[i * t_sub + t_sub // 2 - 1:i * t_sub + t_sub // 2, :] for i in range(n_sub)]
        anchor_rows = jnp.concatenate([jnp.broadcast_to(a, (t_sub, a.shape[1])) for a in anchors], axis=0)
        q_intra = (q * jnp.exp(jnp.minimum(b - anchor_rows, EXP_CLAMP))).astype(BF16)
        q_inter = (q * jnp.exp(b)).astype(BF16)
        b_last = b[c - 1:c, :]
        k_state = (k * jnp.exp(b_last - b)).astype(BF16)
        decay = jnp.exp(b_last)
        k_parts = []
        for i, a in enumerate(anchors):
            hi = (i + 1) * t_sub
            k_parts.append((k[:hi] * jnp.exp(jnp.minimum(a - b[:hi], EXP_CLAMP))).astype(BF16))
            if hi < c:
                k_parts.append(jnp.zeros((c - hi, k.shape[1]), BF16))
        k_intra = jnp.concatenate(k_parts, axis=0)

        n_grp = 2 if n_sub % 2 == 0 else 1
        per = n_sub // n_grp
        a_mat = []
        for h in heads:
            blocks = []
            for g in range(n_grp):
                qg = q_intra[g * per * t_sub:(g + 1) * per * t_sub, ks[h]]
                w = _dot_nt(qg, k_intra[g * per * c:(g + 1) * per * c, ks[h]])
                blocks += [w[li * t_sub:(li + 1) * t_sub, li * c:(li + 1) * c] for li in range(per)]
            a_mat.append(jnp.concatenate(blocks, axis=0))
        a_mat = [jnp.where(ss <= tt, a, 0.0).astype(BF16) for a in a_mat]
        o = [_dot_nt(q_inter[:, ks[h]], st[h].astype(BF16)) + _dot(a_mat[h], v16[:, vs[h]]) for h in heads]
        st = [decay[:, ks[h]] * st[h] + _dot_tn(v16[:, vs[h]], k_state[:, ks[h]]) for h in heads]
        outs = [_rms(o[h], g_ref[...]) * sg_ref[0, rows, vs[h]].astype(F32) for h in heads]
        o_ref[0, rows, :] = jnp.concatenate(outs, axis=1).astype(o_ref.dtype)
    for h in heads:
        st_sc[h] = st[h]

    @pl.when(c_idx == pl.num_programs(1) - 1)
    def _():
        for h in range(n_heads):
            s_ref[0, h] = st_sc[h].T


def _hgrn_prompt(qb, kb, vb, lf, sg, norm_g, *, n_heads):
    b, l, wk = qb.shape
    wv = vb.shape[2]
    c = _pick_tile(l, HGRN_CHUNK)
    step = c * math.gcd(l // c, HGRN_STEP_CHUNKS)
    dk, dv = wk // n_heads, wv // n_heads

    def tok(width):
        return pl.BlockSpec((1, step, width), lambda bb, cc: (bb, cc, 0))

    return pl.pallas_call(
        functools.partial(_hgrn_kernel, n_heads=n_heads, c=c),
        grid=(b, l // step),
        in_specs=[tok(wk), tok(wk), tok(wv), tok(wk), tok(wv), pl.BlockSpec(norm_g.shape, lambda bb, cc: (0, 0))],
        out_specs=[tok(wv), pl.BlockSpec((1, n_heads, dk, dv), lambda bb, cc: (bb, 0, 0, 0))],
        out_shape=[jax.ShapeDtypeStruct((b, l, wv), BF16), jax.ShapeDtypeStruct((b, n_heads, dk, dv), F32)],
        scratch_shapes=[pltpu.VMEM((n_heads, dv, dk), F32)],
        compiler_params=_cparams(("parallel", "arbitrary")),
        name="hgrn_prompt",
    )(qb, kb, vb, lf, sg, norm_g)


def _hgrn_decode_kernel(q_ref, k_ref, v_ref, lf_ref, sg_ref, g_ref, s0_ref, o_ref, s_ref, *, n_heads):
    n = q_ref.shape[1]
    dk = q_ref.shape[2] // n_heads
    dv = v_ref.shape[2] // n_heads
    row = lax.broadcasted_iota(jnp.int32, (n, 1), 0)
    pad = 2 * 8 - n if n < 2 * 8 else 0
    for bi, h in [(bi, h) for bi in range(q_ref.shape[0]) for h in range(n_heads)]:
        ks = slice(h * dk, (h + 1) * dk)
        vs = slice(h * dv, (h + 1) * dv)
        q = q_ref[bi, :, ks].astype(F32)
        k = k_ref[bi, :, ks].astype(F32)
        v = v_ref[bi, :, vs].astype(F32)
        lf = lf_ref[bi, :, ks]
        st = s0_ref[bi, h].T
        b = lf
        for sh in [1 << e for e in range((n - 1).bit_length())]:
            shifted = jnp.concatenate([jnp.zeros((sh, dk), F32), b[:n - sh]], axis=0)
            b = b + shifted

        def padded(x):
            return jnp.concatenate([x, jnp.zeros((pad, x.shape[1]), F32)], axis=0) if pad else x

        o = _dot_nt(padded(q * jnp.exp(b)).astype(BF16), st.astype(BF16))[:n]
        for s in range(n):
            w = jnp.sum(q * k[s:s + 1] * jnp.exp(jnp.minimum(b - b[s:s + 1], 0.0)), axis=-1, keepdims=True)
            o = o + jnp.where(row >= s, w, 0.0) * v[s:s + 1]
        b_last = b[n - 1:n]
        kd = padded(k * jnp.exp(b_last - b)).astype(BF16)
        st_new = jnp.exp(b_last) * st + _dot_tn(padded(v).astype(BF16), kd)
        s_ref[bi, h] = st_new.T
        o_ref[bi, :, vs] = (_rms(o, g_ref[...]) * sg_ref[bi, :, vs].astype(F32)).astype(o_ref.dtype)


def _hgrn_decode(qb, kb, vb, lf, sg, norm_g, state, layer, *, n_heads):
    b, n, wk = qb.shape
    wv = vb.shape[2]
    dk, dv = wk // n_heads, wv // n_heads

    nb = math.gcd(b, HGRN_DECODE_SEQS)

    def tok(width):
        return pl.BlockSpec((nb, n, width), lambda bb: (bb, 0, 0))

    st = pl.BlockSpec((nb, n_heads, dk, dv), lambda bb: (bb, 0, 0, 0))
    st_in = pl.BlockSpec((None, nb, n_heads, dk, dv), lambda bb: (layer, bb, 0, 0, 0))
    return pl.pallas_call(
        functools.partial(_hgrn_decode_kernel, n_heads=n_heads),
        grid=(b // nb,),
        in_specs=[tok(wk), tok(wk), tok(wv), tok(wk), tok(wv), pl.BlockSpec(norm_g.shape, lambda bb: (0, 0)), st_in],
        out_specs=[tok(wv), st],
        out_shape=[jax.ShapeDtypeStruct((b, n, wv), BF16), jax.ShapeDtypeStruct((b, n_heads, dk, dv), F32)],
        compiler_params=_cparams(("parallel",)),
        name="hgrn_decode",
    )(qb, kb, vb, lf, sg, norm_g, state)


def _merge_kernel(h_ref, oa_ref, ob_ref, ga_ref, gb_ref, wa_ref, wb_ref, wo_ref, g_ref, o_ref):
    m = (jax.nn.sigmoid(ga_ref[...].astype(F32)) * _dot(oa_ref[...], wa_ref[...])
         + jax.nn.sigmoid(gb_ref[...].astype(F32)) * _dot(ob_ref[...], wb_ref[...]))
    o_ref[...] = h_ref[...] + _rms(_dot(m.astype(BF16), wo_ref[...]), g_ref[...])


def _merge(h, oa, ob, ga, gb, w_proj_a, w_proj_b, w_out, post_g):
    n, d = h.shape
    tm = _pick_tile(n, MERGE_ROWS)

    def row(width):
        return pl.BlockSpec((tm, width), lambda i: (i, 0))

    return pl.pallas_call(
        _merge_kernel,
        grid=(n // tm,),
        in_specs=[row(d), row(oa.shape[1]), row(ob.shape[1]), row(d), row(d),
                  _resident(w_proj_a.shape), _resident(w_proj_b.shape), _resident(w_out.shape), _resident((1, d))],
        out_specs=row(d),
        out_shape=jax.ShapeDtypeStruct((n, d), F32),
        compiler_params=_cparams(("parallel",)),
        name="merge",
    )(h, oa, ob, ga, gb, w_proj_a, w_proj_b, w_out, post_g)


def kernel(x_prompt, x_sample, cache_k, cache_v, state_s, page_table, ffn1_pre_g, ffn1_post_g, ffn1_w_gu, ffn1_w_down, mix_pre_g, mix_post_g, w_in, lambda_q1, lambda_k1, lambda_q2, lambda_k2, attn_subln_g, hgrn_lb_logits, hgrn_norm_g, w_proj_a, w_proj_b, w_out, ffn2_pre_g, ffn2_post_g, ffn2_w_gu, ffn2_w_down):
    depth, d_model = ffn1_pre_g.shape
    bp, lp, _ = x_prompt.shape
    bs, ls, _ = x_sample.shape
    n_heads_a, two_dqk = cache_k.shape[3], cache_k.shape[4]
    d_qk = two_dqk // 2
    d_v_a = cache_v.shape[4]
    n_heads_b, d_k_b, d_v_b = state_s.shape[2:]
    w_qk, w_a = n_heads_a * two_dqk, n_heads_a * d_v_a
    w_kb, w_b = n_heads_b * d_k_b, n_heads_b * d_v_b
    widths = (w_qk, w_qk, w_a, w_kb, w_kb, w_b, w_b, d_model, d_model)
    assert sum(widths) == w_in.shape[2] and two_dqk == d_v_a == LANES and cache_k.shape[2] == PAGE_SIZE

    cache_k2 = cache_k.reshape(depth, cache_k.shape[1], PAGE_SIZE * n_heads_a, two_dqk)
    cache_v2 = cache_v.reshape(depth, cache_v.shape[1], PAGE_SIZE * n_heads_a, d_v_a)
    yp = x_prompt.reshape(bp * lp, d_model)
    ys = x_sample.reshape(bs * ls, d_model)
    kp_l, vp_l, sp_l, ks_l, vs_l, ss_l = [], [], [], [], [], []
    for l in range(depth):
        lam_init = 0.8 - 0.6 * math.exp(-0.3 * l)
        row = lambda g: g[l].reshape(1, -1)
        wgu1, wdn1 = ffn1_w_gu[l].astype(BF16), ffn1_w_down[l].astype(BF16)
        wgu2, wdn2 = ffn2_w_gu[l].astype(BF16), ffn2_w_down[l].astype(BF16)
        win = w_in[l].astype(BF16)
        wpa, wpb, wo = w_proj_a[l].astype(BF16), w_proj_b[l].astype(BF16), w_out[l].astype(BF16)
        lam_params = jnp.stack([lambda_q1[l], lambda_k1[l], lambda_q2[l], lambda_k2[l]])
        subln_g, norm_g = row(attn_subln_g), row(hgrn_norm_g)

        def mixer_inputs(y, seq_len):
            h = _ffn(y, row(ffn1_pre_g), row(ffn1_post_g), wgu1, wdn1)
            return h, _proj(h, row(mix_pre_g), win, hgrn_lb_logits, widths, l, d_qk ** -0.5 * LOG2E,
                            n_heads_a, seq_len)

        def mixer_outputs(h, oa, ob, ga, gb):
            h2 = _merge(h, oa, ob, ga, gb, wpa, wpb, wo, row(mix_post_g))
            return _ffn(h2, row(ffn2_pre_g), row(ffn2_post_g), wgu2, wdn2)

        h, (qa, ka, ka16, va, vat, qb, lf, kb, vb, sg, ga, gb) = mixer_inputs(yp, lp)
        seq = lambda a: a.reshape(bp, lp, a.shape[1])
        oa = _attn_prompt(seq(qa), seq(ka16), vat, lam_params, subln_g,
                          n_heads=n_heads_a, d_qk=d_qk, lam_init=lam_init)
        ob, sp = _hgrn_prompt(seq(qb), seq(kb), seq(vb), seq(lf), seq(sg), norm_g, n_heads=n_heads_b)
        yp = mixer_outputs(h, oa.reshape(bp * lp, w_a), ob.reshape(bp * lp, w_b), ga, gb)
        kp_l.append(ka.reshape(bp, lp, n_heads_a, two_dqk))
        vp_l.append(va.reshape(bp, lp, n_heads_a, d_v_a))
        sp_l.append(sp)

        h, (qa, ka, _, va, _, qb, lf, kb, vb, sg, ga, gb) = mixer_inputs(ys, ls)
        seq = lambda a: a.reshape(bs, ls, a.shape[1])
        ka_s = ka.reshape(bs, ls, n_heads_a, two_dqk)
        va_s = va.reshape(bs, ls, n_heads_a, d_v_a)
        oa = _attn_decode(seq(qa), ka_s, va_s, cache_k2, cache_v2, l, page_table, lam_params, subln_g,
                          n_heads=n_heads_a, d_qk=d_qk, lam_init=lam_init)
        ob, ss = _hgrn_decode(seq(qb), seq(kb), seq(vb), seq(lf), seq(sg), norm_g, state_s, l, n_heads=n_heads_b)
        ys = mixer_outputs(h, oa.reshape(bs * ls, w_a), ob.reshape(bs * ls, w_b), ga, gb)
        ks_l.append(ka_s)
        vs_l.append(va_s)
        ss_l.append(ss)

    def layers(xs):
        return xs[0][None] if len(xs) == 1 else jnp.stack(xs)

    return (yp.reshape(bp, lp, d_model), ys.reshape(bs, ls, d_model),
            layers(kp_l), layers(vp_l), layers(sp_l), layers(ks_l), layers(vs_l), layers(ss_l))
```
